```python
import math
import jax, jax.numpy as jnp
from jax import lax
import numpy as np

D_MODEL = 1024
BATCH = 4
SEQ = 4096
DEPTH = 4
DEC_BATCH = 128
DEC_SEQ = 8
PAST_LEN = 2048
PAGE_SIZE = 128

N_MIXERS = 4
CONV_WIDTH = 31
D_CONV = D_MODEL
DA_HEADS = 8
DA_HEAD_DIM = D_MODEL // (2 * DA_HEADS)
Q_BLOCK = 128
POOL_WINDOWS = (2, 4, 8, 16)
POOL_GROUPS = len(POOL_WINDOWS)
POOL_GROUP_DIM = D_MODEL // POOL_GROUPS
POOL_BUF = max(POOL_WINDOWS) - 1
SG_CHUNK = 128
SG_DIM = 2 * D_MODEL
SG_GROUPS = 4
SG_GROUP_DIM = SG_DIM // SG_GROUPS
D_FF = 2816
FFN_CONV_WIDTH = 3
NORM_EPS = 1e-6

kernel_name = 'hybrid_conv_diffattn_pool_sgmlp_decode_step'


def _rmsnorm(x, g):
    xf = x.astype(jnp.float32)
    y = xf * lax.rsqrt(jnp.mean(xf * xf, axis=-1, keepdims=True) + NORM_EPS)
    return (y * g.astype(jnp.float32)).astype(x.dtype)


def _layernorm(x, g, b):
    xf = x.astype(jnp.float32)
    mu = jnp.mean(xf, axis=-1, keepdims=True)
    xc = xf - mu
    y = xc * lax.rsqrt(jnp.mean(xc * xc, axis=-1, keepdims=True) + NORM_EPS)
    return (y * g.astype(jnp.float32) + b.astype(jnp.float32)).astype(x.dtype)


def _causal_dwconv(xp, w, b):
    c = xp.shape[-1]
    y = lax.conv_general_dilated(xp, w[:, None, :].astype(xp.dtype), window_strides=(1,), padding='VALID',
                                 dimension_numbers=('NWC', 'WIO', 'NWC'), feature_group_count=c)
    return y + b


def _conformer_conv(h, buf, w_in, b_in, w_dw, b_dw, ln_g, ln_b, w_out, b_out):
    a, gate = jnp.split(h @ w_in + b_in, 2, axis=-1)
    g = a * jax.nn.sigmoid(gate)
    gp = jnp.concatenate([buf, g], axis=1)
    c = _causal_dwconv(gp, w_dw, b_dw)
    c = jax.nn.silu(_layernorm(c, ln_g, ln_b))
    return c @ w_out + b_out, gp[:, -(CONV_WIDTH - 1):]


def _diff_lambda(lq1, lk1, lq2, lk2, layer_idx):
    lam_init = 0.8 - 0.6 * math.exp(-0.3 * layer_idx)
    f = jnp.float32
    lam = (jnp.exp(jnp.sum(lq1.astype(f) * lk1.astype(f))) - jnp.exp(jnp.sum(lq2.astype(f) * lk2.astype(f)))
           + lam_init)
    return lam, lam_init


def _qkv(h, w_qkv):
    b, t, _ = h.shape
    q, k, v = jnp.split(h @ w_qkv, 3, axis=-1)
    shp = (b, t, DA_HEADS, 2 * DA_HEAD_DIM)
    return q.reshape(shp), k.reshape(shp), v.reshape(shp)


def _two_scores(q, k):
    d = DA_HEAD_DIM
    scale = d ** -0.5
    s1 = jnp.einsum('bqhd,bkhd->bhqk', q[..., :d], k[..., :d], preferred_element_type=jnp.float32) * scale
    s2 = jnp.einsum('bqhd,bkhd->bhqk', q[..., d:], k[..., d:], preferred_element_type=jnp.float32) * scale
    return s1, s2


def _diff_weights(s1, s2, lam):
    return jax.nn.softmax(s1, axis=-1) - lam * jax.nn.softmax(s2, axis=-1)


def _diff_attn_prompt(q, k, v, lam):
    b, s, h, e = q.shape
    nb = s // Q_BLOCK
    qb = q.reshape(b, nb, Q_BLOCK, h, e).swapaxes(0, 1)
    starts = jnp.arange(nb, dtype=jnp.int32) * Q_BLOCK
    k_pos = jnp.arange(s, dtype=jnp.int32)

    def block(args):
        qi, start = args
        s1, s2 = _two_scores(qi, k)
        q_pos = start + jnp.arange(Q_BLOCK, dtype=jnp.int32)
        mask = k_pos[None, :] <= q_pos[:, None]
        s1 = jnp.where(mask, s1, -jnp.inf)
        s2 = jnp.where(mask, s2, -jnp.inf)
        w = _diff_weights(s1, s2, lam).astype(v.dtype)
        return jnp.einsum('bhqk,bkhe->bqhe', w, v)

    o = lax.map(block, (qb, starts))
    return o.swapaxes(0, 1).reshape(b, s, h, e)


def _diff_attn_sample(q, k_new, v_new, k_past, v_past, lam):
    t = q.shape[1]
    p = k_past.shape[1]
    s1p, s2p = _two_scores(q, k_past)
    s1n, s2n = _two_scores(q, k_new)
    causal = jnp.tril(jnp.ones((t, t), dtype=bool))
    s1 = jnp.concatenate([s1p, jnp.where(causal, s1n, -jnp.inf)], axis=-1)
    s2 = jnp.concatenate([s2p, jnp.where(causal, s2n, -jnp.inf)], axis=-1)
    w = _diff_weights(s1, s2, lam).astype(v_new.dtype)
    return (jnp.einsum('bhqk,bkhe->bqhe', w[..., :p], v_past)
            + jnp.einsum('bhqk,bkhe->bqhe', w[..., p:], v_new))


def _diff_out(o, lam_init, norm_g, w_o):
    b, t, h, e = o.shape
    of = o.astype(jnp.float32)
    of = of * lax.rsqrt(jnp.mean(of * of, axis=-1, keepdims=True) + NORM_EPS)
    of = of * norm_g.astype(jnp.float32).reshape(h, e) * (1.0 - lam_init)
    return of.astype(o.dtype).reshape(b, t, h * e) @ w_o


def _pool_mixer(h, buf, pos0, w_grp, scale):
    b, t, d = h.shape
    hcat = jnp.concatenate([buf, h], axis=1)
    hf = hcat.astype(jnp.float32)
    cs = jnp.concatenate([jnp.zeros((b, 1, d), jnp.float32), jnp.cumsum(hf, axis=1)], axis=1)
    end = cs[:, POOL_BUF + 1:]
    xs = hf[:, POOL_BUF:]
    pos = pos0 + jnp.arange(t, dtype=jnp.int32)
    outs = []
    for gi, win in enumerate(POOL_WINDOWS):
        sl = slice(gi * POOL_GROUP_DIM, (gi + 1) * POOL_GROUP_DIM)
        start = cs[:, POOL_BUF + 1 - win: POOL_BUF + 1 - win + t, sl]
        cnt = jnp.minimum(win, pos + 1).astype(jnp.float32)[None, :, None]
        outs.append((end[..., sl] - start) / cnt - xs[..., sl])
    pooled = jnp.stack(outs, axis=2).astype(h.dtype)
    y = jnp.einsum('btgc,gce->btge', pooled, w_grp).reshape(b, t, d) * scale
    return y, hcat[:, -POOL_BUF:]


def _sg_mixer(h, w_in, b_in, ln_g, ln_b, w_s, b_s, w_out):
    b, t, _ = h.shape
    z = jax.nn.gelu(h @ w_in + b_in)
    u, v = jnp.split(z, 2, axis=-1)
    v = _layernorm(v, ln_g, ln_b)
    L = SG_CHUNK if t >= SG_CHUNK else t
    ws = w_s[:, :L, :L] * jnp.tril(jnp.ones((L, L), w_s.dtype))
    vc = v.reshape(b, t // L, L, SG_GROUPS, SG_GROUP_DIM)
    s = jnp.einsum('gts,bnsgc->bntgc', ws, vc) + b_s[:, :L].T[:, :, None]
    y = (u * s.reshape(b, t, SG_DIM)) @ w_out
    return y, v


def _conv_ffn(h, buf, w_gate, w_up, w_dw, b_dw, w_down):
    g = h @ w_gate
    u = h @ w_up
    gp = jnp.concatenate([buf, g], axis=1)
    gc = _causal_dwconv(gp, w_dw, b_dw)
    return (jax.nn.silu(gc) * u) @ w_down, gp[:, -(FFN_CONV_WIDTH - 1):]


def setup_inputs(seed: int = 0) -> dict:
    key = jax.random.key(seed)
    keys = jax.random.split(key, 48)
    ctr = [0]

    def nk():
        ctr[0] += 1
        return keys[ctr[0] - 1]

    def nrm(shape, scale):
        return jax.random.normal(nk(), shape, jnp.float32) * scale

    def gain(shape):
        return 1.0 + nrm(shape, 0.05)

    n_pages = PAST_LEN // PAGE_SIZE
    n_used = DEC_BATCH * n_pages
    n_phys = n_used + n_used // 4
    kv_shape = (n_phys, PAGE_SIZE, DA_HEADS, 2 * DA_HEAD_DIM)
    d = D_MODEL
    inp = {}
    inp['x_prompt'] = nrm((BATCH, SEQ, d), 1.0)
    inp['x_sample'] = nrm((DEC_BATCH, DEC_SEQ, d), 1.0)
    inp['state_conv'] = nrm((DEC_BATCH, CONV_WIDTH - 1, D_CONV), 0.5)
    inp['cache_k'] = nrm(kv_shape, 1.0)
    inp['cache_v'] = nrm(kv_shape, 1.0)
    inp['page_table'] = jax.random.permutation(nk(), n_phys)[:n_used].reshape(DEC_BATCH, n_pages).astype(jnp.int32)
    inp['state_pool'] = nrm((DEC_BATCH, POOL_BUF, d), 1.0)
    inp['state_ffn'] = nrm((DEPTH, DEC_BATCH, FFN_CONV_WIDTH - 1, D_FF), 1.0)
    inp['norm_mix'] = gain((DEPTH, d))
    inp['norm_ffn'] = gain((DEPTH, d))
    inp['norm_final'] = gain((d,))
    inp['cv_w_in'] = nrm((d, 2 * D_CONV), d ** -0.5)
    inp['cv_b_in'] = nrm((2 * D_CONV,), 0.02)
    inp['cv_w_dw'] = nrm((CONV_WIDTH, D_CONV), CONV_WIDTH ** -0.5)
    inp['cv_b_dw'] = nrm((D_CONV,), 0.02)
    inp['cv_ln_g'] = gain((D_CONV,))
    inp['cv_ln_b'] = nrm((D_CONV,), 0.02)
    inp['cv_w_out'] = nrm((D_CONV, d), D_CONV ** -0.5)
    inp['cv_b_out'] = nrm((d,), 0.02)
    inp['da_w_qkv'] = nrm((d, 3 * DA_HEADS * 2 * DA_HEAD_DIM), d ** -0.5)
    inp['da_lq1'] = nrm((DA_HEAD_DIM,), 0.1)
    inp['da_lk1'] = nrm((DA_HEAD_DIM,), 0.1)
    inp['da_lq2'] = nrm((DA_HEAD_DIM,), 0.1)
    inp['da_lk2'] = nrm((DA_HEAD_DIM,), 0.1)
    inp['da_norm_g'] = gain((DA_HEADS * 2 * DA_HEAD_DIM,))
    inp['da_w_o'] = nrm((DA_HEADS * 2 * DA_HEAD_DIM, d), d ** -0.5)
    inp['pl_w'] = nrm((POOL_GROUPS, POOL_GROUP_DIM, POOL_GROUP_DIM), POOL_GROUP_DIM ** -0.5)
    inp['pl_scale'] = 1.0 + nrm((d,), 0.1)
    inp['sg_w_in'] = nrm((d, 2 * SG_DIM), d ** -0.5)
    inp['sg_b_in'] = nrm((2 * SG_DIM,), 0.02)
    inp['sg_ln_g'] = gain((SG_DIM,))
    inp['sg_ln_b'] = nrm((SG_DIM,), 0.02)
    inp['sg_w_s'] = nrm((SG_GROUPS, SG_CHUNK, SG_CHUNK), SG_CHUNK ** -0.5)
    inp['sg_b_s'] = 1.0 + nrm((SG_GROUPS, SG_CHUNK), 0.1)
    inp['sg_w_out'] = nrm((SG_DIM, d), SG_DIM ** -0.5)
    inp['ff_w_gate'] = nrm((DEPTH, d, D_FF), d ** -0.5)
    inp['ff_w_up'] = nrm((DEPTH, d, D_FF), d ** -0.5)
    inp['ff_w_dw'] = nrm((DEPTH, FFN_CONV_WIDTH, D_FF), FFN_CONV_WIDTH ** -0.5)
    inp['ff_b_dw'] = nrm((DEPTH, D_FF), 0.02)
    inp['ff_w_down'] = nrm((DEPTH, D_FF, d), D_FF ** -0.5)
    return inp


def reference(x_prompt, x_sample, state_conv, cache_k, cache_v, page_table, state_pool, state_ffn,
              norm_mix, norm_ffn, norm_final,
              cv_w_in, cv_b_in, cv_w_dw, cv_b_dw, cv_ln_g, cv_ln_b, cv_w_out, cv_b_out,
              da_w_qkv, da_lq1, da_lk1, da_lq2, da_lk2, da_norm_g, da_w_o,
              pl_w, pl_scale,
              sg_w_in, sg_b_in, sg_ln_g, sg_ln_b, sg_w_s, sg_b_s, sg_w_out,
              ff_w_gate, ff_w_up, ff_w_dw, ff_b_dw, ff_w_down):
    xp, xs = x_prompt, x_sample
    bp, bs = xp.shape[0], xs.shape[0]
    kv_row = (bs, -1, DA_HEADS, 2 * DA_HEAD_DIM)
    ffn_p_list, ffn_s_list = [], []
    for i in range(DEPTH):
        kind = i % N_MIXERS
        hp = _rmsnorm(xp, norm_mix[i])
        hs = _rmsnorm(xs, norm_mix[i])
        if kind == 0:
            cw = (cv_w_in, cv_b_in, cv_w_dw, cv_b_dw, cv_ln_g, cv_ln_b, cv_w_out, cv_b_out)
            mp, conv_p = _conformer_conv(hp, jnp.zeros((bp, CONV_WIDTH - 1, D_CONV), hp.dtype), *cw)
            ms, conv_s = _conformer_conv(hs, state_conv, *cw)
        elif kind == 1:
            lam, lam_init = _diff_lambda(da_lq1, da_lk1, da_lq2, da_lk2, i)
            q_p, k_rows_p, v_rows_p = _qkv(hp, da_w_qkv)
            mp = _diff_out(_diff_attn_prompt(q_p, k_rows_p, v_rows_p, lam), lam_init, da_norm_g, da_w_o)
            q_s, k_rows_s, v_rows_s = _qkv(hs, da_w_qkv)
            k_past = cache_k[page_table].reshape(kv_row)
            v_past = cache_v[page_table].reshape(kv_row)
            ms = _diff_out(_diff_attn_sample(q_s, k_rows_s, v_rows_s, k_past, v_past, lam), lam_init,
                           da_norm_g, da_w_o)
        elif kind == 2:
            mp, pool_p = _pool_mixer(hp, jnp.zeros((bp, POOL_BUF, D_MODEL), hp.dtype), 0, pl_w, pl_scale)
            ms, pool_s = _pool_mixer(hs, state_pool, PAST_LEN, pl_w, pl_scale)
        else:
            sw = (sg_w_in, sg_b_in, sg_ln_g, sg_ln_b, sg_w_s, sg_b_s, sg_w_out)
            mp, _ = _sg_mixer(hp, *sw)
            ms, sg_v_s = _sg_mixer(hs, *sw)
        xp = xp + mp
        xs = xs + ms
        fw = (ff_w_gate[i], ff_w_up[i], ff_w_dw[i], ff_b_dw[i], ff_w_down[i])
        fp, st_p = _conv_ffn(_rmsnorm(xp, norm_ffn[i]), jnp.zeros((bp, FFN_CONV_WIDTH - 1, D_FF), xp.dtype), *fw)
        fs, st_s = _conv_ffn(_rmsnorm(xs, norm_ffn[i]), state_ffn[i], *fw)
        xp = xp + fp
        xs = xs + fs
        ffn_p_list.append(st_p)
        ffn_s_list.append(st_s)
    y_prompt = _rmsnorm(xp, norm_final)
    y_sample = _rmsnorm(xs, norm_final)
    ffn_p = jnp.stack(ffn_p_list, axis=0)
    ffn_s = jnp.stack(ffn_s_list, axis=0)
    return (y_prompt, y_sample, conv_p, conv_s, k_rows_p, v_rows_p, k_rows_s, v_rows_s,
            pool_p, pool_s, sg_v_s, ffn_p, ffn_s)
```

```python
import functools
import math

import jax
import jax.numpy as jnp
from jax import lax
from jax.experimental import pallas as pl
from jax.experimental.pallas import tpu as pltpu

F32 = jnp.float32
BF16 = jnp.bfloat16
NORM_EPS = 1e-6
POOL_WINDOWS = (2, 4, 8, 16)
SG_CHUNK = 128
SG_GROUPS = 4
DA_HEADS = 8

SUBLANES = 8
LANES = 128
MXU_DIM = 256
VMEM_LIMIT_BYTES = 56 * 1024 * 1024
ROWS_PER_STEP = 512


def _round_up(n, m):
    return -(-n // m) * m


def _plan_bt(batch, seq, short_rows):
    if seq >= ROWS_PER_STEP:
        assert seq % ROWS_PER_STEP == 0
        return 1, ROWS_PER_STEP
    bb = max(1, min(batch, short_rows // seq))
    assert batch % bb == 0 and seq % SUBLANES == 0
    return bb, seq


def _const_spec(shape):
    nd = len(shape)
    return pl.BlockSpec(shape, lambda *_: (0,) * nd, pipeline_mode=pl.Buffered(1))


def _params(n_grid):
    return pltpu.CompilerParams(dimension_semantics=("arbitrary",) * n_grid,
                                vmem_limit_bytes=VMEM_LIMIT_BYTES)


def _rms_rows(x, g):
    return x * lax.rsqrt(jnp.mean(x * x, axis=-1, keepdims=True) + NORM_EPS) * g


def _layernorm_rows(x, g, b):
    mu = jnp.mean(x, axis=-1, keepdims=True)
    xc = x - mu
    return xc * lax.rsqrt(jnp.mean(xc * xc, axis=-1, keepdims=True) + NORM_EPS) * g + b


def _sigmoid(x):
    return 1.0 / (1.0 + jnp.exp(-x))


def _silu(x):
    return x * _sigmoid(x)


def _gelu_tanh(x):
    return x * (0.5 * (1.0 + jnp.tanh(math.sqrt(2.0 / math.pi) * (x + 0.044715 * (x * x * x)))))


def _idiv(x, n):
    assert n & (n - 1) == 0
    return x >> (n.bit_length() - 1)


def _imod(x, n):
    assert n & (n - 1) == 0
    return x & (n - 1)


def _dot(a, b):
    return jnp.dot(a, b, preferred_element_type=F32)


def _diff_lambda(lq1_ref, lk1_ref, lq2_ref, lk2_ref, lam_init):
    a = jnp.sum(lq1_ref[...] * lk1_ref[...], axis=-1, keepdims=True)
    b = jnp.sum(lq2_ref[...] * lk2_ref[...], axis=-1, keepdims=True)
    return jnp.exp(a) - jnp.exp(b) + lam_init


def _conv_mixer_kernel(x_ref, st_ref, g_ref, win_ref, bin_ref, wdw_ref, bdw_ref, lng_ref, lnb_ref,
                       wout_ref, bout_ref, o_ref, sto_ref, gp_ref, c_ref, *, bb, tt, kw, hp, sb, rb, cb):
    hist = kw - 1
    t = pl.program_id(1)
    d = x_ref.shape[-1]
    c = gp_ref.shape[-1]
    rows = bb * tt

    @pl.when(t == 0)
    def _():
        gp_ref[:, hp - hist:hp, :] = st_ref[...]

    x = x_ref[...].reshape(rows, d)
    h = _rms_rows(x, g_ref[...]).astype(BF16)
    ag = _dot(h, win_ref[...]) + bin_ref[...]
    glu = ag[:, :c] * _sigmoid(ag[:, c:])
    gp_ref[:, hp:hp + tt, :] = glu.reshape(bb, tt, c)

    base = hp - hist
    for b0 in range(0, bb, sb):
        for r0 in range(0, tt, rb):
            for c0 in range(0, c, cb):
                acc = jnp.broadcast_to(bdw_ref[:, c0:c0 + cb].reshape(1, 1, cb), (sb, rb, cb))
                for k in range(kw):
                    win = gp_ref[b0:b0 + sb, base + r0 + k:base + r0 + k + rb, c0:c0 + cb]
                    acc = acc + win * wdw_ref[k:k + 1, c0:c0 + cb].reshape(1, 1, cb)
                c_ref[b0:b0 + sb, r0:r0 + rb, c0:c0 + cb] = acc

    conv = c_ref[...].reshape(rows, c)
    act = _silu(_layernorm_rows(conv, lng_ref[...], lnb_ref[...])).astype(BF16)
    m = _dot(act, wout_ref[...]) + bout_ref[...]
    o_ref[...] = (x + m).reshape(bb, tt, d)

    new_hist = gp_ref[:, tt + hp - hist:tt + hp, :]
    gp_ref[:, hp - hist:hp, :] = new_hist

    @pl.when(t == pl.num_programs(1) - 1)
    def _():
        sto_ref[...] = new_hist


def _conv_mixer(x, state, g, w_in, b_in, w_dw, b_dw, ln_g, ln_b, w_out, b_out):
    batch, seq, d = x.shape
    kw, c = w_dw.shape
    hist = kw - 1
    hp = _round_up(hist, SUBLANES)
    bb, tt = _plan_bt(batch, seq, short_rows=ROWS_PER_STEP // 4)
    cb = 2 * LANES
    rb = min(tt, LANES)
    sb = max(1, min(bb, LANES // rb))
    kern = functools.partial(_conv_mixer_kernel, bb=bb, tt=tt, kw=kw, hp=hp, sb=sb, rb=rb, cb=cb)
    x_spec = pl.BlockSpec((bb, tt, d), lambda b, t: (b, t, 0))
    st_spec = pl.BlockSpec((bb, hist, c), lambda b, t: (b, 0, 0))
    return pl.pallas_call(
        kern,
        grid=(batch // bb, seq // tt),
        in_specs=[x_spec, st_spec, _const_spec((1, d)), _const_spec((d, 2 * c)), _const_spec((1, 2 * c)),
                  _const_spec((kw, c)), _const_spec((1, c)), _const_spec((1, c)), _const_spec((1, c)),
                  _const_spec((c, d)), _const_spec((1, d))],
        out_specs=[x_spec, st_spec],
        out_shape=[jax.ShapeDtypeStruct((batch, seq, d), F32), jax.ShapeDtypeStruct((batch, hist, c), F32)],
        scratch_shapes=[pltpu.VMEM((bb, hp + tt, c), F32), pltpu.VMEM((bb, tt, c), F32)],
        compiler_params=_params(2),
        name="conv_mixer",
    )(x, state, g.reshape(1, d), w_in.astype(BF16), b_in.reshape(1, 2 * c), w_dw, b_dw.reshape(1, c),
      ln_g.reshape(1, c), ln_b.reshape(1, c), w_out.astype(BF16), b_out.reshape(1, d))


def _ffn_kernel(*refs, bb, tt, proj, final, chunks, hp):
    refs = list(refs)
    x_ref = refs.pop(0)
    if proj:
        m_ref = refs.pop(0)
        wp_ref = refs.pop(0)
    st_ref, g_ref, wg_ref, wu_ref, wdw_ref, bdw_ref, wd_ref = refs[:7]
    refs = refs[7:]
    if final:
        gf_ref = refs.pop(0)
    o_ref, sto_ref, gpad_ref, carry_ref = refs
    hist = wdw_ref.shape[0] - 1
    t = pl.program_id(1)
    d = x_ref.shape[-1]
    rows = bb * tt

    @pl.when(t == 0)
    def _():
        carry_ref[...] = st_ref[...]

    x = x_ref[...].reshape(rows, d)
    if proj:
        x = x + _dot(m_ref[...].reshape(rows, m_ref.shape[-1]).astype(BF16), wp_ref[...])
    h = _rms_rows(x, g_ref[...]).astype(BF16)
    acc = x
    base = hp - hist
    for c0, fc in chunks:
        gate = _dot(h, wg_ref[:, c0:c0 + fc]).reshape(bb, tt, fc)
        up = _dot(h, wu_ref[:, c0:c0 + fc])
        gpad_ref[:, base:hp, :fc] = carry_ref[:, :, c0:c0 + fc]
        gpad_ref[:, hp:hp + tt, :fc] = gate
        gc = gate * wdw_ref[hist:hist + 1, c0:c0 + fc].reshape(1, 1, fc) + bdw_ref[:, c0:c0 + fc].reshape(1, 1, fc)
        for k in range(hist):
            gc = gc + gpad_ref[:, base + k:base + k + tt, :fc] * wdw_ref[k:k + 1, c0:c0 + fc].reshape(1, 1, fc)
        carry_ref[:, :, c0:c0 + fc] = gpad_ref[:, tt + base:tt + hp, :fc]
        act = (_silu(gc).reshape(rows, fc) * up).astype(BF16)
        acc = acc + _dot(act, wd_ref[c0:c0 + fc, :])
    if final:
        acc = _rms_rows(acc, gf_ref[...])
    o_ref[...] = acc.reshape(bb, tt, d)
    sto_ref[...] = carry_ref[...]


def _ffn(x, state, g, w_gate, w_up, w_dw, b_dw, w_down, proj=None, final_g=None):
    batch, seq, d = x.shape
    kw, f = w_dw.shape
    hist = kw - 1
    hp = _round_up(hist, SUBLANES)
    bb, tt = _plan_bt(batch, seq, short_rows=ROWS_PER_STEP // 2)
    n_tiles = f // MXU_DIM
    assert f % MXU_DIM == 0
    half = (n_tiles + 1) // 2 * MXU_DIM
    chunks = ((0, half), (half, f - half)) if f > half else ((0, f),)
    kern = functools.partial(_ffn_kernel, bb=bb, tt=tt, proj=proj is not None, final=final_g is not None,
                             chunks=chunks, hp=hp)
    x_spec = pl.BlockSpec((bb, tt, d), lambda b, t: (b, t, 0))
    st_spec = pl.BlockSpec((bb, hist, f), lambda b, t: (b, 0, 0))
    args, specs = [x], [x_spec]
    if proj is not None:
        m, w_proj = proj
        dm = m.shape[-1]
        args += [m, w_proj.astype(BF16)]
        specs += [pl.BlockSpec((bb, tt, dm), lambda b, t: (b, t, 0)), _const_spec((dm, d))]
    args += [state, g.reshape(1, d), w_gate.astype(BF16), w_up.astype(BF16), w_dw, b_dw.reshape(1, f),
             w_down.astype(BF16)]
    specs += [st_spec, _const_spec((1, d)), _const_spec((d, f)), _const_spec((d, f)), _const_spec((kw, f)),
              _const_spec((1, f)), _const_spec((f, d))]
    if final_g is not None:
        args.append(final_g.reshape(1, d))
        specs.append(_const_spec((1, d)))
    return pl.pallas_call(
        kern,
        grid=(batch // bb, seq // tt),
        in_specs=specs,
        out_specs=[x_spec, st_spec],
        out_shape=[jax.ShapeDtypeStruct((batch, seq, d), F32), jax.ShapeDtypeStruct((batch, hist, f), F32)],
        scratch_shapes=[pltpu.VMEM((bb, hp + tt, chunks[0][1]), F32), pltpu.VMEM((bb, hist, f), F32)],
        compiler_params=_params(2),
        name="conv_ffn",
    )(*args)


def _qkv_kernel(x_ref, g_ref, w_ref, q_ref, k_ref, v_ref, *maybe_bf, scale):
    dq = q_ref.shape[-1]
    h = _rms_rows(x_ref[...], g_ref[...]).astype(BF16)
    qkv = _dot(h, w_ref[...])
    q_ref[...] = (qkv[:, :dq] * scale).astype(q_ref.dtype)
    k = qkv[:, dq:2 * dq]
    v = qkv[:, 2 * dq:]
    k_ref[...] = k
    v_ref[...] = v
    if maybe_bf:
        kb_ref, vb_ref = maybe_bf
        kb_ref[...] = k.astype(BF16)
        vb_ref[...] = v.astype(BF16)


def _qkv(x2d, g, w_qkv, scale, q_dtype, emit_bf16):
    n, d = x2d.shape
    dq = w_qkv.shape[1] // 3
    tm = min(n, ROWS_PER_STEP)
    row_spec = pl.BlockSpec((tm, dq), lambda i: (i, 0))
    out_shape = [jax.ShapeDtypeStruct((n, dq), q_dtype), jax.ShapeDtypeStruct((n, dq), F32),
                 jax.ShapeDtypeStruct((n, dq), F32)]
    if emit_bf16:
        out_shape += [jax.ShapeDtypeStruct((n, dq), BF16)] * 2
    return pl.pallas_call(
        functools.partial(_qkv_kernel, scale=scale),
        grid=(n // tm,),
        in_specs=[pl.BlockSpec((tm, d), lambda i: (i, 0)), _const_spec((1, d)), _const_spec((d, 3 * dq))],
        out_specs=[row_spec] * len(out_shape),
        out_shape=out_shape,
        compiler_params=_params(1),
        name="qkv",
    )(x2d, g.reshape(1, d), w_qkv.astype(BF16))


def _subln(o, ng, lam_init):
    return o * lax.rsqrt(jnp.mean(o * o, axis=-1, keepdims=True) + NORM_EPS) * ng * (1.0 - lam_init)


def _attn_prompt_kernel(lq1_ref, lk1_ref, lq2_ref, lk2_ref, ng_ref, q_ref, k_ref, v_ref, o_ref, *, tq, lam_init):
    qi = pl.program_id(2)
    e = q_ref.shape[-1]
    q = q_ref[...]
    lane = lax.broadcasted_iota(jnp.int32, q.shape, 1)
    zero = jnp.zeros_like(q)
    qq = jnp.concatenate([jnp.where(lane < e // 2, q, zero), jnp.where(lane >= e // 2, q, zero)], axis=0)

    def step(j, carry, on_diagonal):
        m, l, acc = carry
        start = pl.multiple_of(j * tq, tq)
        kt = k_ref[pl.ds(start, tq), :]
        vt = v_ref[pl.ds(start, tq), :]
        s = lax.dot_general(qq, kt, (((1,), (1,)), ((), ())), preferred_element_type=F32)
        if on_diagonal:
            row = lax.broadcasted_iota(jnp.int32, s.shape, 0)
            col = lax.broadcasted_iota(jnp.int32, s.shape, 1)
            s = jnp.where(col <= jnp.where(row >= tq, row - tq, row), s, -jnp.inf)
        m_new = jnp.maximum(m, jnp.max(s, axis=-1, keepdims=True))
        p = jnp.exp(s - m_new)
        alpha = jnp.exp(m - m_new)
        l = alpha * l + jnp.sum(p, axis=-1, keepdims=True)
        acc = alpha * acc + _dot(p.astype(BF16), vt)
        return m_new, l, acc

    init = (jnp.full((2 * tq, 1), -jnp.inf, F32), jnp.zeros((2 * tq, 1), F32), jnp.zeros((2 * tq, e), F32))
    carry = lax.fori_loop(0, qi, lambda j, c: step(j, c, False), init)
    _, l, acc = step(qi, carry, True)
    o = acc * (1.0 / l)
    lam = _diff_lambda(lq1_ref, lk1_ref, lq2_ref, lk2_ref, lam_init)
    o = o[:tq] - lam * o[tq:]
    o_ref[...] = _subln(o, ng_ref[...], lam_init).astype(o_ref.dtype)


def _attn_prompt(q, k, v, lqk, norm_g, lam_init, tq):
    batch, seq, dq = q.shape
    e = dq // DA_HEADS
    dl = lqk[0].shape[0]
    q_spec = pl.BlockSpec((None, tq, e), lambda b, h, i: (b, i, h))
    kv_spec = pl.BlockSpec((None, seq, e), lambda b, h, i: (b, 0, h))
    return pl.pallas_call(
        functools.partial(_attn_prompt_kernel, tq=tq, lam_init=lam_init),
        grid=(batch, DA_HEADS, seq // tq),
        in_specs=[_const_spec((1, dl))] * 4 + [pl.BlockSpec((1, e), lambda b, h, i: (0, h)), q_spec, kv_spec, kv_spec],
        out_specs=q_spec,
        out_shape=jax.ShapeDtypeStruct((batch, seq, dq), BF16),
        compiler_params=_params(3),
        name="attn_prompt",
    )(*[a.reshape(1, dl) for a in lqk], norm_g.reshape(1, dq), q, k, v)


def _attn_sample_kernel(pt_ref, lq1_ref, lk1_ref, lq2_ref, lk2_ref, ng_ref, q_ref, kn_ref, vn_ref, *rest,
                        n_pages, page, lam_init):
    del pt_ref
    k_pages = rest[:n_pages]
    v_pages = rest[n_pages:2 * n_pages]
    o_ref, kbuf_ref, vbuf_ref = rest[2 * n_pages:]
    t, dq = q_ref.shape
    e = dq // DA_HEADS
    past = n_pages * page
    pad = kbuf_ref.shape[0] - past
    new_rows = 2 * SUBLANES
    n_rows = DA_HEADS * 2 * t
    assert t == SUBLANES and n_rows == LANES

    @pl.when(pl.program_id(0) == 0)
    def _():
        kbuf_ref[past + new_rows:, :] = jnp.zeros((pad - new_rows, dq), BF16)
        vbuf_ref[past + new_rows:, :] = jnp.zeros((pad - new_rows, dq), BF16)

    for p in range(n_pages):
        kbuf_ref[p * page:(p + 1) * page, :] = k_pages[p][...].astype(BF16)
        vbuf_ref[p * page:(p + 1) * page, :] = v_pages[p][...].astype(BF16)
    fill = jnp.zeros((new_rows - t, dq), F32)
    kbuf_ref[past:past + new_rows, :] = jnp.concatenate([kn_ref[...], fill], axis=0).astype(BF16)
    vbuf_ref[past:past + new_rows, :] = jnp.concatenate([vn_ref[...], fill], axis=0).astype(BF16)

    qt = jnp.concatenate([q_ref[...]] * (n_rows // t), axis=0)
    r = lax.broadcasted_iota(jnp.int32, qt.shape, 0)
    c = lax.broadcasted_iota(jnp.int32, qt.shape, 1)
    same_head = _idiv(c, e) == _idiv(r, 2 * t)
    same_half = _imod(_idiv(c, e // 2), 2) == _imod(_idiv(r, t), 2)
    wt = jnp.where(same_head, jnp.where(same_half, qt, 0.0), 0.0).astype(BF16)
    s = lax.dot_general(wt, kbuf_ref[...], (((1,), (1,)), ((), ())), preferred_element_type=F32)
    row = lax.broadcasted_iota(jnp.int32, s.shape, 0)
    col = lax.broadcasted_iota(jnp.int32, s.shape, 1)
    s = jnp.where(col - past <= _imod(row, t), s, -jnp.inf)
    m = jnp.max(s, axis=-1, keepdims=True)
    p = jnp.exp(s - m)
    l = jnp.sum(p, axis=-1, keepdims=True)
    of = _dot(p.astype(BF16), vbuf_ref[...]) * (1.0 / l)
    lam = _diff_lambda(lq1_ref, lk1_ref, lq2_ref, lk2_ref, lam_init)
    heads = []
    for h in range(DA_HEADS):
        blk = of[h * 2 * t:(h + 1) * 2 * t, h * e:(h + 1) * e]
        o = blk[:t] - lam * blk[t:]
        heads.append(_subln(o, ng_ref[:, h * e:(h + 1) * e], lam_init))
    o_ref[...] = jnp.concatenate(heads, axis=-1)


def _attn_sample(q, k_new, v_new, cache_k, cache_v, page_table, lqk, norm_g, lam_init):
    batch, t, dq = q.shape
    n_phys, page = cache_k.shape[:2]
    n_pages = page_table.shape[1]
    dl = lqk[0].shape[0]
    ck = cache_k.reshape(n_phys, page, dq)
    cv = cache_v.reshape(n_phys, page, dq)
    row_spec = pl.BlockSpec((None, t, dq), lambda b, pt: (b, 0, 0))

    def page_spec(p):
        return pl.BlockSpec((None, page, dq), lambda b, pt: (pt[b, p], 0, 0))

    page_specs = [page_spec(p) for p in range(n_pages)]
    buf_rows = n_pages * page + LANES
    grid_spec = pltpu.PrefetchScalarGridSpec(
        num_scalar_prefetch=1,
        grid=(batch,),
        in_specs=[_const_spec((1, dl))] * 4 + [_const_spec((1, dq)), row_spec, row_spec, row_spec]
        + page_specs + page_specs,
        out_specs=row_spec,
        scratch_shapes=[pltpu.VMEM((buf_rows, dq), BF16), pltpu.VMEM((buf_rows, dq), BF16)],
    )
    return pl.pallas_call(
        functools.partial(_attn_sample_kernel, n_pages=n_pages, page=page, lam_init=lam_init),
        grid_spec=grid_spec,
        out_shape=jax.ShapeDtypeStruct((batch, t, dq), F32),
        compiler_params=_params(1),
        name="attn_sample",
    )(page_table, *[a.reshape(1, dl) for a in lqk], norm_g.reshape(1, dq), q, k_new, v_new,
      *([ck] * n_pages), *([cv] * n_pages))


def _pool_kernel(x_ref, st_ref, g_ref, w_ref, sc_ref, o_ref, sto_ref, hp_ref, *, bb, tt, windows, pos0, hist, hp):
    t = pl.program_id(1)
    d = x_ref.shape[-1]
    cg = d // len(windows)
    rows = bb * tt

    @pl.when(t == 0)
    def _():
        hp_ref[:, hp - hist:hp, :] = st_ref[...]

    x = x_ref[...].reshape(rows, d)
    h = _rms_rows(x, g_ref[...]).reshape(bb, tt, d)
    hp_ref[:, hp:hp + tt, :] = h
    pos = pos0 + t * tt + lax.broadcasted_iota(jnp.int32, (1, tt, 1), 1)
    ys = []
    for gi, win in enumerate(windows):
        c0 = gi * cg
        cur = h[:, :, c0:c0 + cg]
        tot = cur
        for j in range(1, win):
            tot = tot + hp_ref[:, hp - j:hp - j + tt, c0:c0 + cg]
        cnt = jnp.minimum(win, pos + 1).astype(F32)
        pooled = (tot / cnt - cur).reshape(rows, cg).astype(BF16)
        ys.append(_dot(pooled, w_ref[gi]))
    y = jnp.concatenate(ys, axis=-1) * sc_ref[...]
    o_ref[...] = (x + y).reshape(bb, tt, d)

    new_hist = hp_ref[:, tt + hp - hist:tt + hp, :]
    hp_ref[:, hp - hist:hp, :] = new_hist

    @pl.when(t == pl.num_programs(1) - 1)
    def _():
        sto_ref[...] = new_hist


def _pool_mixer(x, state, g, w_grp, scale, pos0):
    batch, seq, d = x.shape
    hist = max(POOL_WINDOWS) - 1
    hp = _round_up(hist, SUBLANES)
    n_grp, cg, _ = w_grp.shape
    bb, tt = _plan_bt(batch, seq, short_rows=ROWS_PER_STEP // 2)
    x_spec = pl.BlockSpec((bb, tt, d), lambda b, t: (b, t, 0))
    st_spec = pl.BlockSpec((bb, hist, d), lambda b, t: (b, 0, 0))
    return pl.pallas_call(
        functools.partial(_pool_kernel, bb=bb, tt=tt, windows=POOL_WINDOWS, pos0=pos0, hist=hist, hp=hp),
        grid=(batch // bb, seq // tt),
        in_specs=[x_spec, st_spec, _const_spec((1, d)), _const_spec((n_grp, cg, cg)), _const_spec((1, d))],
        out_specs=[x_spec, st_spec],
        out_shape=[jax.ShapeDtypeStruct((batch, seq, d), F32), jax.ShapeDtypeStruct((batch, hist, d), F32)],
        scratch_shapes=[pltpu.VMEM((bb, hp + tt, d), F32)],
        compiler_params=_params(2),
        name="pool_mixer",
    )(x, state, g.reshape(1, d), w_grp.astype(BF16), scale.reshape(1, d))


def _sg_kernel(x_ref, g_ref, win_ref, bin_ref, lng_ref, lnb_ref, ws_ref, bs_ref, wout_ref, o_ref, *maybe_v,
               tm, chunk):
    sg = wout_ref.shape[0]
    cg = sg // SG_GROUPS
    x = x_ref[...]
    h = _rms_rows(x, g_ref[...]).astype(BF16)
    z = _gelu_tanh(_dot(h, win_ref[...]) + bin_ref[...])
    u = z[:, :sg]
    v = _layernorm_rows(z[:, sg:], lng_ref[...], lnb_ref[...])
    if maybe_v:
        maybe_v[0][...] = v
    vb = v.astype(BF16)
    r = lax.broadcasted_iota(jnp.int32, (SG_CHUNK, SG_CHUNK), 0)
    c = lax.broadcasted_iota(jnp.int32, (SG_CHUNK, SG_CHUNK), 1)
    keep = (_idiv(r, chunk) == _idiv(c, chunk)) & (c <= r)
    acc = x
    for gi in range(SG_GROUPS):
        ws = jnp.where(keep, ws_ref[gi], 0.0).astype(BF16)
        parts = []
        for r0 in range(0, tm, SG_CHUNK):
            parts.append(_dot(ws, vb[r0:r0 + SG_CHUNK, gi * cg:(gi + 1) * cg]) + bs_ref[gi])
        s = jnp.concatenate(parts, axis=0) if len(parts) > 1 else parts[0]
        gated = (u[:, gi * cg:(gi + 1) * cg] * s).astype(BF16)
        acc = acc + _dot(gated, wout_ref[gi * cg:(gi + 1) * cg, :])
    o_ref[...] = acc


def _sg_mixer(x2d, g, w_in, b_in, ln_g, ln_b, w_s, b_s, w_out, chunk, emit_v, tm):
    n, d = x2d.shape
    sg = w_out.shape[0]
    reps = SG_CHUNK // chunk
    ws = jnp.tile(w_s[:, :chunk, :chunk], (1, reps, reps))
    bs = jnp.tile(b_s[:, :chunk], (1, reps)).reshape(SG_GROUPS, SG_CHUNK, 1)
    row_spec = pl.BlockSpec((tm, d), lambda i: (i, 0))
    out_shape = [jax.ShapeDtypeStruct((n, d), F32)]
    out_specs = [row_spec]
    if emit_v:
        out_shape.append(jax.ShapeDtypeStruct((n, sg), F32))
        out_specs.append(pl.BlockSpec((tm, sg), lambda i: (i, 0)))
    return pl.pallas_call(
        functools.partial(_sg_kernel, tm=tm, chunk=chunk),
        grid=(n // tm,),
        in_specs=[row_spec, _const_spec((1, d)), _const_spec((d, 2 * sg)), _const_spec((1, 2 * sg)),
                  _const_spec((1, sg)), _const_spec((1, sg)), _const_spec((SG_GROUPS, SG_CHUNK, SG_CHUNK)),
                  _const_spec((SG_GROUPS, SG_CHUNK, 1)), _const_spec((sg, d))],
        out_specs=out_specs,
        out_shape=out_shape,
        compiler_params=_params(1),
        name="sg_mixer",
    )(x2d, g.reshape(1, d), w_in.astype(BF16), b_in.reshape(1, 2 * sg), ln_g.reshape(1, sg), ln_b.reshape(1, sg),
      ws, bs, w_out.astype(BF16))


def kernel(x_prompt, x_sample, state_conv, cache_k, cache_v, page_table, state_pool, state_ffn, norm_mix, norm_ffn, norm_final, cv_w_in, cv_b_in, cv_w_dw, cv_b_dw, cv_ln_g, cv_ln_b, cv_w_out, cv_b_out, da_w_qkv, da_lq1, da_lk1, da_lq2, da_lk2, da_norm_g, da_w_o, pl_w, pl_scale, sg_w_in, sg_b_in, sg_ln_g, sg_ln_b, sg_w_s, sg_b_s, sg_w_out, ff_w_gate, ff_w_up, ff_w_dw, ff_b_dw, ff_w_down):
    bp, seq, d = x_prompt.shape
    bs, dec_seq, _ = x_sample.shape
    depth, ffn_kw, d_ff = ff_w_dw.shape
    past_len = page_table.shape[1] * cache_k.shape[1]
    head_dim = d // (2 * DA_HEADS)
    kv_shape = (DA_HEADS, 2 * head_dim)

    def ffn(i, x, state, **kw):
        return _ffn(x, state, norm_ffn[i], ff_w_gate[i], ff_w_up[i], ff_w_dw[i], ff_b_dw[i], ff_w_down[i], **kw)

    ffn_zero = jnp.zeros((bp, ffn_kw - 1, d_ff), F32)
    ffn_p, ffn_s = [], []

    cw = (norm_mix[0], cv_w_in, cv_b_in, cv_w_dw, cv_b_dw, cv_ln_g, cv_ln_b, cv_w_out, cv_b_out)
    xp, conv_p = _conv_mixer(x_prompt, jnp.zeros((bp,) + state_conv.shape[1:], F32), *cw)
    xs, conv_s = _conv_mixer(x_sample, state_conv, *cw)
    xp, st = ffn(0, xp, ffn_zero)
    ffn_p.append(st)
    xs, st = ffn(0, xs, state_ffn[0])
    ffn_s.append(st)

    lam_init = 0.8 - 0.6 * math.exp(-0.3 * 1)
    lqk = (da_lq1, da_lk1, da_lq2, da_lk2)
    scale = head_dim ** -0.5
    assert math.log2(scale).is_integer()
    q_p, k_rows_p, v_rows_p, kb_p, vb_p = _qkv(xp.reshape(bp * seq, d), norm_mix[1], da_w_qkv, scale, BF16, True)
    o_p = _attn_prompt(q_p.reshape(bp, seq, d), kb_p.reshape(bp, seq, d), vb_p.reshape(bp, seq, d), lqk,
                       da_norm_g, lam_init, tq=min(seq, 2 * LANES))
    xp, st = ffn(1, xp, ffn_zero, proj=(o_p, da_w_o))
    ffn_p.append(st)
    q_s, k_rows_s, v_rows_s = _qkv(xs.reshape(bs * dec_seq, d), norm_mix[1], da_w_qkv, scale, F32, False)
    o_s = _attn_sample(q_s.reshape(bs, dec_seq, d), k_rows_s.reshape(bs, dec_seq, d),
                       v_rows_s.reshape(bs, dec_seq, d), cache_k, cache_v, page_table, lqk, da_norm_g, lam_init)
    xs, st = ffn(1, xs, state_ffn[1], proj=(o_s, da_w_o))
    ffn_s.append(st)

    xp, pool_p = _pool_mixer(xp, jnp.zeros((bp,) + state_pool.shape[1:], F32), norm_mix[2], pl_w, pl_scale, 0)
    xs, pool_s = _pool_mixer(xs, state_pool, norm_mix[2], pl_w, pl_scale, past_len)
    xp, st = ffn(2, xp, ffn_zero)
    ffn_p.append(st)
    xs, st = ffn(2, xs, state_ffn[2])
    ffn_s.append(st)

    sw = (norm_mix[3], sg_w_in, sg_b_in, sg_ln_g, sg_ln_b, sg_w_s, sg_b_s, sg_w_out)
    assert seq % SG_CHUNK == 0 and past_len % SG_CHUNK == 0 and SG_CHUNK % dec_seq == 0
    (xp2,) = _sg_mixer(xp.reshape(bp * seq, d), *sw, chunk=SG_CHUNK, emit_v=False, tm=2 * LANES)
    xs2, sg_v = _sg_mixer(xs.reshape(bs * dec_seq, d), *sw, chunk=dec_seq, emit_v=True, tm=2 * LANES)
    y_prompt, st = ffn(3, xp2.reshape(bp, seq, d), ffn_zero, final_g=norm_final)
    ffn_p.append(st)
    y_sample, st = ffn(3, xs2.reshape(bs, dec_seq, d), state_ffn[3], final_g=norm_final)
    ffn_s.append(st)

    return (y_prompt, y_sample, conv_p, conv_s,
            k_rows_p.reshape((bp, seq) + kv_shape), v_rows_p.reshape((bp, seq) + kv_shape),
            k_rows_s.reshape((bs, dec_seq) + kv_shape), v_rows_s.reshape((bs, dec_seq) + kv_shape),
            pool_p, pool_s, sg_v.reshape(bs, dec_seq, -1), jnp.stack(ffn_p, axis=0), jnp.stack(ffn_s, axis=0))
```

```python
import functools
import math

import jax
import jax.numpy as jnp
from jax import lax
from jax.experimental import pallas as pl
from jax.experimental.pallas import tpu as pltpu

F32 = jnp.float32
BF16 = jnp.bfloat16
NORM_EPS = 1e-6
POOL_WINDOWS = (2, 4, 8, 16)
SG_CHUNK = 128
SG_GROUPS = 4
DA_HEADS = 8

SUBLANES = 8
LANES = 128
MXU_DIM = 256
VMEM_LIMIT_BYTES = 56 * 1024 * 1024
ROWS_PER_STEP = 512
ATTN_TQ = 512
ATTN_TK = 1024


def _round_up(n, m):
    return -(-n // m) * m


def _plan_bt(batch, seq, short_rows):
    if seq >= ROWS_PER_STEP:
        assert seq % ROWS_PER_STEP == 0
        return 1, ROWS_PER_STEP
    bb = max(1, min(batch, short_rows // seq))
    assert batch % bb == 0 and seq % SUBLANES == 0
    return bb, seq


def _const_spec(shape):
    nd = len(shape)
    return pl.BlockSpec(shape, lambda *_: (0,) * nd, pipeline_mode=pl.Buffered(1))


def _params(n_grid):
    return pltpu.CompilerParams(dimension_semantics=("arbitrary",) * n_grid,
                                vmem_limit_bytes=VMEM_LIMIT_BYTES)


def _rms_rows(x, g):
    return x * lax.rsqrt(jnp.mean(x * x, axis=-1, keepdims=True) + NORM_EPS) * g


def _layernorm_rows(x, g, b):
    mu = jnp.mean(x, axis=-1, keepdims=True)
    xc = x - mu
    return xc * lax.rsqrt(jnp.mean(xc * xc, axis=-1, keepdims=True) + NORM_EPS) * g + b


def _sigmoid(x):
    return 1.0 / (1.0 + jnp.exp(-x))


def _silu(x):
    return x * _sigmoid(x)


def _gelu_tanh(x):
    return x * (0.5 * (1.0 + jnp.tanh(math.sqrt(2.0 / math.pi) * (x + 0.044715 * (x * x * x)))))


def _idiv(x, n):
    assert n & (n - 1) == 0
    return x >> (n.bit_length() - 1)


def _imod(x, n):
    assert n & (n - 1) == 0
    return x & (n - 1)


def _dot(a, b):
    return jnp.dot(a, b, preferred_element_type=F32)


def _diff_lambda(lq1_ref, lk1_ref, lq2_ref, lk2_ref, lam_init):
    a = jnp.sum(lq1_ref[...] * lk1_ref[...], axis=-1, keepdims=True)
    b = jnp.sum(lq2_ref[...] * lk2_ref[...], axis=-1, keepdims=True)
    return jnp.exp(a) - jnp.exp(b) + lam_init


def _conv_mixer_kernel(x_ref, st_ref, g_ref, win_ref, bin_ref, wdw_ref, bdw_ref, lng_ref, lnb_ref,
                       wout_ref, bout_ref, o_ref, sto_ref, gp_ref, c_ref, *, bb, tt, kw, hp, sb, rb, cb):
    hist = kw - 1
    t = pl.program_id(1)
    d = x_ref.shape[-1]
    c = gp_ref.shape[-1]
    rows = bb * tt

    @pl.when(t == 0)
    def _():
        gp_ref[:, hp - hist:hp, :] = st_ref[...]

    x = x_ref[...].reshape(rows, d)
    h = _rms_rows(x, g_ref[...]).astype(BF16)
    ag = _dot(h, win_ref[...]) + bin_ref[...]
    glu = ag[:, :c] * _sigmoid(ag[:, c:])
    gp_ref[:, hp:hp + tt, :] = glu.reshape(bb, tt, c)

    base = hp - hist
    for b0 in range(0, bb, sb):
        for r0 in range(0, tt, rb):
            for c0 in range(0, c, cb):
                acc = jnp.broadcast_to(bdw_ref[:, c0:c0 + cb].reshape(1, 1, cb), (sb, rb, cb))
                for k in range(kw):
                    win = gp_ref[b0:b0 + sb, base + r0 + k:base + r0 + k + rb, c0:c0 + cb]
                    acc = acc + win * wdw_ref[k:k + 1, c0:c0 + cb].reshape(1, 1, cb)
                c_ref[b0:b0 + sb, r0:r0 + rb, c0:c0 + cb] = acc

    conv = c_ref[...].reshape(rows, c)
    act = _silu(_layernorm_rows(conv, lng_ref[...], lnb_ref[...])).astype(BF16)
    m = _dot(act, wout_ref[...]) + bout_ref[...]
    o_ref[...] = (x + m).reshape(bb, tt, d)

    new_hist = gp_ref[:, tt + hp - hist:tt + hp, :]
    gp_ref[:, hp - hist:hp, :] = new_hist

    @pl.when(t == pl.num_programs(1) - 1)
    def _():
        sto_ref[...] = new_hist


def _conv_mixer(x, state, g, w_in, b_in, w_dw, b_dw, ln_g, ln_b, w_out, b_out):
    batch, seq, d = x.shape
    kw, c = w_dw.shape
    hist = kw - 1
    hp = _round_up(hist, SUBLANES)
    bb, tt = _plan_bt(batch, seq, short_rows=ROWS_PER_STEP // 4)
    cb = 2 * LANES
    rb = min(tt, LANES)
    sb = max(1, min(bb, LANES // rb))
    kern = functools.partial(_conv_mixer_kernel, bb=bb, tt=tt, kw=kw, hp=hp, sb=sb, rb=rb, cb=cb)
    x_spec = pl.BlockSpec((bb, tt, d), lambda b, t: (b, t, 0))
    st_spec = pl.BlockSpec((bb, hist, c), lambda b, t: (b, 0, 0))
    return pl.pallas_call(
        kern,
        grid=(batch // bb, seq // tt),
        in_specs=[x_spec, st_spec, _const_spec((1, d)), _const_spec((d, 2 * c)), _const_spec((1, 2 * c)),
                  _const_spec((kw, c)), _const_spec((1, c)), _const_spec((1, c)), _const_spec((1, c)),
                  _const_spec((c, d)), _const_spec((1, d))],
        out_specs=[x_spec, st_spec],
        out_shape=[jax.ShapeDtypeStruct((batch, seq, d), F32), jax.ShapeDtypeStruct((batch, hist, c), F32)],
        scratch_shapes=[pltpu.VMEM((bb, hp + tt, c), F32), pltpu.VMEM((bb, tt, c), F32)],
        compiler_params=_params(2),
        name="conv_mixer",
    )(x, state, g.reshape(1, d), w_in.astype(BF16), b_in.reshape(1, 2 * c), w_dw, b_dw.reshape(1, c),
      ln_g.reshape(1, c), ln_b.reshape(1, c), w_out.astype(BF16), b_out.reshape(1, d))


def _ffn_kernel(*refs, bb, tt, proj, final, chunks, hp):
    refs = list(refs)
    x_ref = refs.pop(0)
    if proj:
        m_ref = refs.pop(0)
        wp_ref = refs.pop(0)
    st_ref, g_ref, wg_ref, wu_ref, wdw_ref, bdw_ref, wd_ref = refs[:7]
    refs = refs[7:]
    if final:
        gf_ref = refs.pop(0)
    o_ref, sto_ref, gpad_ref, carry_ref = refs
    hist = wdw_ref.shape[0] - 1
    t = pl.program_id(1)
    d = x_ref.shape[-1]
    rows = bb * tt

    @pl.when(t == 0)
    def _():
        carry_ref[...] = st_ref[...]

    x = x_ref[...].reshape(rows, d)
    if proj:
        x = x + _dot(m_ref[...].reshape(rows, m_ref.shape[-1]).astype(BF16), wp_ref[...])
    h = _rms_rows(x, g_ref[...]).astype(BF16)
    acc = x
    base = hp - hist
    for c0, fc in chunks:
        gate = _dot(h, wg_ref[:, c0:c0 + fc]).reshape(bb, tt, fc)
        up = _dot(h, wu_ref[:, c0:c0 + fc])
        gpad_ref[:, base:hp, :fc] = carry_ref[:, :, c0:c0 + fc]
        gpad_ref[:, hp:hp + tt, :fc] = gate
        gc = gate * wdw_ref[hist:hist + 1, c0:c0 + fc].reshape(1, 1, fc) + bdw_ref[:, c0:c0 + fc].reshape(1, 1, fc)
        for k in range(hist):
            gc = gc + gpad_ref[:, base + k:base + k + tt, :fc] * wdw_ref[k:k + 1, c0:c0 + fc].reshape(1, 1, fc)
        carry_ref[:, :, c0:c0 + fc] = gpad_ref[:, tt + base:tt + hp, :fc]
        act = (_silu(gc).reshape(rows, fc) * up).astype(BF16)
        acc = acc + _dot(act, wd_ref[c0:c0 + fc, :])
    if final:
        acc = _rms_rows(acc, gf_ref[...])
    o_ref[...] = acc.reshape(bb, tt, d)
    sto_ref[...] = carry_ref[...]


def _ffn(x, state, g, w_gate, w_up, w_dw, b_dw, w_down, proj=None, final_g=None):
    batch, seq, d = x.shape
    kw, f = w_dw.shape
    hist = kw - 1
    hp = _round_up(hist, SUBLANES)
    bb, tt = _plan_bt(batch, seq, short_rows=ROWS_PER_STEP // 2)
    n_tiles = f // MXU_DIM
    assert f % MXU_DIM == 0
    half = (n_tiles + 1) // 2 * MXU_DIM
    chunks = ((0, half), (half, f - half)) if f > half else ((0, f),)
    kern = functools.partial(_ffn_kernel, bb=bb, tt=tt, proj=proj is not None, final=final_g is not None,
                             chunks=chunks, hp=hp)
    x_spec = pl.BlockSpec((bb, tt, d), lambda b, t: (b, t, 0))
    st_spec = pl.BlockSpec((bb, hist, f), lambda b, t: (b, 0, 0))
    args, specs = [x], [x_spec]
    if proj is not None:
        m, w_proj = proj
        dm = m.shape[-1]
        args += [m, w_proj.astype(BF16)]
        specs += [pl.BlockSpec((bb, tt, dm), lambda b, t: (b, t, 0)), _const_spec((dm, d))]
    args += [state, g.reshape(1, d), w_gate.astype(BF16), w_up.astype(BF16), w_dw, b_dw.reshape(1, f),
             w_down.astype(BF16)]
    specs += [st_spec, _const_spec((1, d)), _const_spec((d, f)), _const_spec((d, f)), _const_spec((kw, f)),
              _const_spec((1, f)), _const_spec((f, d))]
    if final_g is not None:
        args.append(final_g.reshape(1, d))
        specs.append(_const_spec((1, d)))
    return pl.pallas_call(
        kern,
        grid=(batch // bb, seq // tt),
        in_specs=specs,
        out_specs=[x_spec, st_spec],
        out_shape=[jax.ShapeDtypeStruct((batch, seq, d), F32), jax.ShapeDtypeStruct((batch, hist, f), F32)],
        scratch_shapes=[pltpu.VMEM((bb, hp + tt, chunks[0][1]), F32), pltpu.VMEM((bb, hist, f), F32)],
        compiler_params=_params(2),
        name="conv_ffn",
    )(*args)


def _qkv_kernel(x_ref, g_ref, w_ref, q_ref, k_ref, v_ref, *maybe_bf, scale):
    dq = q_ref.shape[-1]
    h = _rms_rows(x_ref[...], g_ref[...]).astype(BF16)
    qkv = _dot(h, w_ref[...])
    q_ref[...] = (qkv[:, :dq] * scale).astype(q_ref.dtype)
    k = qkv[:, dq:2 * dq]
    v = qkv[:, 2 * dq:]
    k_ref[...] = k
    v_ref[...] = v
    if maybe_bf:
        kb_ref, vb_ref = maybe_bf
        kb_ref[...] = k.astype(BF16)
        vb_ref[...] = v.astype(BF16)


def _qkv(x2d, g, w_qkv, scale, q_dtype, emit_bf16):
    n, d = x2d.shape
    dq = w_qkv.shape[1] // 3
    tm = min(n, ROWS_PER_STEP)
    row_spec = pl.BlockSpec((tm, dq), lambda i: (i, 0))
    out_shape = [jax.ShapeDtypeStruct((n, dq), q_dtype), jax.ShapeDtypeStruct((n, dq), F32),
                 jax.ShapeDtypeStruct((n, dq), F32)]
    if emit_bf16:
        out_shape += [jax.ShapeDtypeStruct((n, dq), BF16)] * 2
    return pl.pallas_call(
        functools.partial(_qkv_kernel, scale=scale),
        grid=(n // tm,),
        in_specs=[pl.BlockSpec((tm, d), lambda i: (i, 0)), _const_spec((1, d)), _const_spec((d, 3 * dq))],
        out_specs=[row_spec] * len(out_shape),
        out_shape=out_shape,
        compiler_params=_params(1),
        name="qkv",
    )(x2d, g.reshape(1, d), w_qkv.astype(BF16))


def _subln(o, ng, lam_init):
    return o * lax.rsqrt(jnp.mean(o * o, axis=-1, keepdims=True) + NORM_EPS) * ng * (1.0 - lam_init)


def _attn_prompt_kernel(lq1_ref, lk1_ref, lq2_ref, lk2_ref, ng_ref, q_ref, k_ref, v_ref, o_ref, *, tq, tk, lam_init):
    qi = pl.program_id(2)
    e = q_ref.shape[-1]
    q = q_ref[...].astype(F32)
    lane = lax.broadcasted_iota(jnp.int32, q.shape, 1)
    qq = jnp.concatenate([jnp.where(lane < e // 2, q, 0.0), jnp.where(lane >= e // 2, q, 0.0)], axis=0)
    qqt = qq.T.astype(BF16)

    def step(j, carry, masked):
        m, l, acc = carry
        start = pl.multiple_of(j * tk, tk)
        kt = k_ref[pl.ds(start, tk), :]
        vt = v_ref[pl.ds(start, tk), :]
        s = _dot(kt, qqt)
        if masked:
            key = lax.broadcasted_iota(jnp.int32, s.shape, 0) + start
            col = lax.broadcasted_iota(jnp.int32, s.shape, 1)
            s = jnp.where(key <= jnp.where(col >= tq, col - tq, col) + qi * tq, s, -jnp.inf)
        m_new = jnp.maximum(m, jnp.max(s, axis=0, keepdims=True))
        p = jnp.exp(s - m_new)
        alpha = jnp.exp(m - m_new)
        l = alpha * l + jnp.sum(p, axis=0, keepdims=True)
        pv = lax.dot_general(vt, p.astype(BF16), (((0,), (0,)), ((), ())), preferred_element_type=F32)
        return m_new, l, alpha * acc + pv

    init = (jnp.full((1, 2 * tq), -jnp.inf, F32), jnp.zeros((1, 2 * tq), F32), jnp.zeros((e, 2 * tq), F32))
    n_full = (qi * tq) // tk
    carry = lax.fori_loop(0, n_full, lambda j, c: step(j, c, False), init)
    _, l, acc = step(n_full, carry, True)
    o = acc * (1.0 / l)
    lam = _diff_lambda(lq1_ref, lk1_ref, lq2_ref, lk2_ref, lam_init)
    o = (o[:, :tq] - lam * o[:, tq:]).T
    o_ref[...] = _subln(o, ng_ref[...], lam_init).astype(o_ref.dtype)


def _attn_prompt(q, k, v, lqk, norm_g, lam_init, tq, tk):
    batch, seq, dq = q.shape
    e = dq // DA_HEADS
    dl = lqk[0].shape[0]
    assert seq % tk == 0 and tk % tq == 0
    q_spec = pl.BlockSpec((None, tq, e), lambda b, h, i: (b, i, h))
    kv_spec = pl.BlockSpec((None, seq, e), lambda b, h, i: (b, 0, h))
    return pl.pallas_call(
        functools.partial(_attn_prompt_kernel, tq=tq, tk=tk, lam_init=lam_init),
        grid=(batch, DA_HEADS, seq // tq),
        in_specs=[_const_spec((1, dl))] * 4 + [pl.BlockSpec((1, e), lambda b, h, i: (0, h)), q_spec, kv_spec, kv_spec],
        out_specs=q_spec,
        out_shape=jax.ShapeDtypeStruct((batch, seq, dq), BF16),
        compiler_params=_params(3),
        name="attn_prompt",
    )(*[a.reshape(1, dl) for a in lqk], norm_g.reshape(1, dq), q, k, v)


def _attn_sample_kernel(pt_ref, lq1_ref, lk1_ref, lq2_ref, lk2_ref, ng_ref, q_ref, kn_ref, vn_ref, *rest,
                        n_pages, lam_init):
    del pt_ref
    k_pages = rest[:n_pages]
    v_pages = rest[n_pages:2 * n_pages]
    o_ref, s_ref = rest[2 * n_pages:]
    t, nh, e = q_ref.shape
    page = k_pages[0].shape[0]
    rows = t * nh
    cols = page * nh
    nt = (((1,), (1,)), ((), ()))

    q2 = q_ref[...].reshape(rows, e)
    lane = lax.broadcasted_iota(jnp.int32, q2.shape, 1)
    wq = jnp.concatenate([jnp.where(lane < e // 2, q2, 0.0), jnp.where(lane >= e // 2, q2, 0.0)], axis=0).astype(BF16)

    r = lax.broadcasted_iota(jnp.int32, (2 * rows, cols), 0)
    c = lax.broadcasted_iota(jnp.int32, (2 * rows, cols), 1)
    head_bias = jnp.where(_imod(c - r, nh) == 0, 0.0, -jnp.inf)

    rn = lax.broadcasted_iota(jnp.int32, (2 * rows, rows), 0)
    cn = lax.broadcasted_iota(jnp.int32, (2 * rows, rows), 1)
    keep_new = (_imod(cn - rn, nh) == 0) & (_idiv(cn, nh) <= _idiv(_imod(rn, rows), nh))
    s_new = lax.dot_general(wq, kn_ref[...].reshape(rows, e).astype(BF16), nt, preferred_element_type=F32)
    s_new = jnp.where(keep_new, s_new, -jnp.inf)

    mx = None
    for p in range(n_pages):
        k2 = k_pages[p][...].reshape(cols, e).astype(BF16)
        s = lax.dot_general(wq, k2, nt, preferred_element_type=F32) + head_bias
        s_ref[:, p * cols:(p + 1) * cols] = s
        mx = s if mx is None else jnp.maximum(mx, s)
    m = jnp.maximum(jnp.max(mx, axis=-1, keepdims=True), jnp.max(s_new, axis=-1, keepdims=True))

    p_new = jnp.exp(s_new - m)
    acc = _dot(p_new.astype(BF16), vn_ref[...].reshape(rows, e).astype(BF16))
    tot = None
    for p in range(n_pages):
        pe = jnp.exp(s_ref[:, p * cols:(p + 1) * cols] - m)
        tot = pe if tot is None else tot + pe
        acc = acc + _dot(pe.astype(BF16), v_pages[p][...].reshape(cols, e).astype(BF16))
    l = jnp.sum(tot, axis=-1, keepdims=True) + jnp.sum(p_new, axis=-1, keepdims=True)

    o = acc * (1.0 / l)
    lam = _diff_lambda(lq1_ref, lk1_ref, lq2_ref, lk2_ref, lam_init)
    o = o[:rows] - lam * o[rows:]
    ng = jnp.concatenate([ng_ref[...]] * t, axis=0)
    o_ref[...] = _subln(o, ng, lam_init).reshape(t, nh, e)


def _attn_sample(q, k_new, v_new, cache_k, cache_v, page_table, lqk, norm_g, lam_init):
    batch, t, nh, e = q.shape
    page = cache_k.shape[1]
    n_pages = page_table.shape[1]
    dl = lqk[0].shape[0]
    row_spec = pl.BlockSpec((None, t, nh, e), lambda b, pt: (b, 0, 0, 0))

    def page_spec(p):
        return pl.BlockSpec((None, page, nh, e), lambda b, pt: (pt[b, p], 0, 0, 0))

    page_specs = [page_spec(p) for p in range(n_pages)]
    grid_spec = pltpu.PrefetchScalarGridSpec(
        num_scalar_prefetch=1,
        grid=(batch,),
        in_specs=[_const_spec((1, dl))] * 4 + [_const_spec((nh, e)), row_spec, row_spec, row_spec]
        + page_specs + page_specs,
        out_specs=row_spec,
        scratch_shapes=[pltpu.VMEM((2 * t * nh, n_pages * page * nh), F32)],
    )
    return pl.pallas_call(
        functools.partial(_attn_sample_kernel, n_pages=n_pages, lam_init=lam_init),
        grid_spec=grid_spec,
        out_shape=jax.ShapeDtypeStruct((batch, t, nh, e), F32),
        compiler_params=_params(1),
        name="attn_sample",
    )(page_table, *[a.reshape(1, dl) for a in lqk], norm_g.reshape(nh, e), q, k_new, v_new,
      *([cache_k] * n_pages), *([cache_v] * n_pages))


def _pool_kernel(x_ref, st_ref, g_ref, w_ref, sc_ref, o_ref, sto_ref, hp_ref, *, bb, tt, windows, pos0, hist, hp):
    t = pl.program_id(1)
    d = x_ref.shape[-1]
    cg = d // len(windows)
    rows = bb * tt

    @pl.when(t == 0)
    def _():
        hp_ref[:, hp - hist:hp, :] = st_ref[...]

    x = x_ref[...].reshape(rows, d)
    h = _rms_rows(x, g_ref[...]).reshape(bb, tt, d)
    hp_ref[:, hp:hp + tt, :] = h
    pos = pos0 + t * tt + lax.broadcasted_iota(jnp.int32, (1, tt, 1), 1)
    ys = []
    for gi, win in enumerate(windows):
        c0 = gi * cg
        cur = h[:, :, c0:c0 + cg]
        tot = cur
        for j in range(1, win):
            tot = tot + hp_ref[:, hp - j:hp - j + tt, c0:c0 + cg]
        cnt = jnp.minimum(win, pos + 1).astype(F32)
        pooled = (tot / cnt - cur).reshape(rows, cg).astype(BF16)
        ys.append(_dot(pooled, w_ref[gi]))
    y = jnp.concatenate(ys, axis=-1) * sc_ref[...]
    o_ref[...] = (x + y).reshape(bb, tt, d)

    new_hist = hp_ref[:, tt + hp - hist:tt + hp, :]
    hp_ref[:, hp - hist:hp, :] = new_hist

    @pl.when(t == pl.num_programs(1) - 1)
    def _():
        sto_ref[...] = new_hist


def _pool_mixer(x, state, g, w_grp, scale, pos0):
    batch, seq, d = x.shape
    hist = max(POOL_WINDOWS) - 1
    hp = _round_up(hist, SUBLANES)
    n_grp, cg, _ = w_grp.shape
    bb, tt = _plan_bt(batch, seq, short_rows=ROWS_PER_STEP // 2)
    x_spec = pl.BlockSpec((bb, tt, d), lambda b, t: (b, t, 0))
    st_spec = pl.BlockSpec((bb, hist, d), lambda b, t: (b, 0, 0))
    return pl.pallas_call(
        functools.partial(_pool_kernel, bb=bb, tt=tt, windows=POOL_WINDOWS, pos0=pos0, hist=hist, hp=hp),
        grid=(batch // bb, seq // tt),
        in_specs=[x_spec, st_spec, _const_spec((1, d)), _const_spec((n_grp, cg, cg)), _const_spec((1, d))],
        out_specs=[x_spec, st_spec],
        out_shape=[jax.ShapeDtypeStruct((batch, seq, d), F32), jax.ShapeDtypeStruct((batch, hist, d), F32)],
        scratch_shapes=[pltpu.VMEM((bb, hp + tt, d), F32)],
        compiler_params=_params(2),
        name="pool_mixer",
    )(x, state, g.reshape(1, d), w_grp.astype(BF16), scale.reshape(1, d))


def _sg_kernel(x_ref, g_ref, win_ref, bin_ref, lng_ref, lnb_ref, ws_ref, bs_ref, wout_ref, o_ref, *maybe_v,
               tm, chunk):
    sg = wout_ref.shape[0]
    cg = sg // SG_GROUPS
    x = x_ref[...]
    h = _rms_rows(x, g_ref[...]).astype(BF16)
    z = _gelu_tanh(_dot(h, win_ref[...]) + bin_ref[...])
    u = z[:, :sg]
    v = _layernorm_rows(z[:, sg:], lng_ref[...], lnb_ref[...])
    if maybe_v:
        maybe_v[0][...] = v
    vb = v.astype(BF16)
    r = lax.broadcasted_iota(jnp.int32, (SG_CHUNK, SG_CHUNK), 0)
    c = lax.broadcasted_iota(jnp.int32, (SG_CHUNK, SG_CHUNK), 1)
    keep = (_idiv(r, chunk) == _idiv(c, chunk)) & (c <= r)
    acc = x
    for gi in range(SG_GROUPS):
        ws = jnp.where(keep, ws_ref[gi], 0.0).astype(BF16)
        parts = []
        for r0 in range(0, tm, SG_CHUNK):
            parts.append(_dot(ws, vb[r0:r0 + SG_CHUNK, gi * cg:(gi + 1) * cg]) + bs_ref[gi])
        s = jnp.concatenate(parts, axis=0) if len(parts) > 1 else parts[0]
        gated = (u[:, gi * cg:(gi + 1) * cg] * s).astype(BF16)
        acc = acc + _dot(gated, wout_ref[gi * cg:(gi + 1) * cg, :])
    o_ref[...] = acc


def _sg_mixer(x2d, g, w_in, b_in, ln_g, ln_b, w_s, b_s, w_out, chunk, emit_v, tm):
    n, d = x2d.shape
    sg = w_out.shape[0]
    reps = SG_CHUNK // chunk
    ws = jnp.tile(w_s[:, :chunk, :chunk], (1, reps, reps))
    bs = jnp.tile(b_s[:, :chunk], (1, reps)).reshape(SG_GROUPS, SG_CHUNK, 1)
    row_spec = pl.BlockSpec((tm, d), lambda i: (i, 0))
    out_shape = [jax.ShapeDtypeStruct((n, d), F32)]
    out_specs = [row_spec]
    if emit_v:
        out_shape.append(jax.ShapeDtypeStruct((n, sg), F32))
        out_specs.append(pl.BlockSpec((tm, sg), lambda i: (i, 0)))
    return pl.pallas_call(
        functools.partial(_sg_kernel, tm=tm, chunk=chunk),
        grid=(n // tm,),
        in_specs=[row_spec, _const_spec((1, d)), _const_spec((d, 2 * sg)), _const_spec((1, 2 * sg)),
                  _const_spec((1, sg)), _const_spec((1, sg)), _const_spec((SG_GROUPS, SG_CHUNK, SG_CHUNK)),
                  _const_spec((SG_GROUPS, SG_CHUNK, 1)), _const_spec((sg, d))],
        out_specs=out_specs,
        out_shape=out_shape,
        compiler_params=_params(1),
        name="sg_mixer",
    )(x2d, g.reshape(1, d), w_in.astype(BF16), b_in.reshape(1, 2 * sg), ln_g.reshape(1, sg), ln_b.reshape(1, sg),
      ws, bs, w_out.astype(BF16))


def kernel(x_prompt, x_sample, state_conv, cache_k, cache_v, page_table, state_pool, state_ffn, norm_mix, norm_ffn, norm_final, cv_w_in, cv_b_in, cv_w_dw, cv_b_dw, cv_ln_g, cv_ln_b, cv_w_out, cv_b_out, da_w_qkv, da_lq1, da_lk1, da_lq2, da_lk2, da_norm_g, da_w_o, pl_w, pl_scale, sg_w_in, sg_b_in, sg_ln_g, sg_ln_b, sg_w_s, sg_b_s, sg_w_out, ff_w_gate, ff_w_up, ff_w_dw, ff_b_dw, ff_w_down):
    bp, seq, d = x_prompt.shape
    bs, dec_seq, _ = x_sample.shape
    depth, ffn_kw, d_ff = ff_w_dw.shape
    past_len = page_table.shape[1] * cache_k.shape[1]
    head_dim = d // (2 * DA_HEADS)
    kv_shape = (DA_HEADS, 2 * head_dim)

    def ffn(i, x, state, **kw):
        return _ffn(x, state, norm_ffn[i], ff_w_gate[i], ff_w_up[i], ff_w_dw[i], ff_b_dw[i], ff_w_down[i], **kw)

    ffn_zero = jnp.zeros((bp, ffn_kw - 1, d_ff), F32)
    ffn_p, ffn_s = [], []

    cw = (norm_mix[0], cv_w_in, cv_b_in, cv_w_dw, cv_b_dw, cv_ln_g, cv_ln_b, cv_w_out, cv_b_out)
    xp, conv_p = _conv_mixer(x_prompt, jnp.zeros((bp,) + state_conv.shape[1:], F32), *cw)
    xs, conv_s = _conv_mixer(x_sample, state_conv, *cw)
    xp, st = ffn(0, xp, ffn_zero)
    ffn_p.append(st)
    xs, st = ffn(0, xs, state_ffn[0])
    ffn_s.append(st)

    lam_init = 0.8 - 0.6 * math.exp(-0.3 * 1)
    lqk = (da_lq1, da_lk1, da_lq2, da_lk2)
    scale = head_dim ** -0.5
    assert math.log2(scale).is_integer()
    q_p, k_rows_p, v_rows_p, kb_p, vb_p = _qkv(xp.reshape(bp * seq, d), norm_mix[1], da_w_qkv, scale, BF16, True)
    o_p = _attn_prompt(q_p.reshape(bp, seq, d), kb_p.reshape(bp, seq, d), vb_p.reshape(bp, seq, d), lqk,
                       da_norm_g, lam_init, tq=min(seq, ATTN_TQ), tk=min(seq, ATTN_TK))
    xp, st = ffn(1, xp, ffn_zero, proj=(o_p, da_w_o))
    ffn_p.append(st)
    q_s, k_rows_s, v_rows_s = _qkv(xs.reshape(bs * dec_seq, d), norm_mix[1], da_w_qkv, scale, F32, False)
    rows_s = (bs, dec_seq) + kv_shape
    o_s = _attn_sample(q_s.reshape(rows_s), k_rows_s.reshape(rows_s), v_rows_s.reshape(rows_s),
                       cache_k, cache_v, page_table, lqk, da_norm_g, lam_init)
    xs, st = ffn(1, xs, state_ffn[1], proj=(o_s.reshape(bs, dec_seq, d), da_w_o))
    ffn_s.append(st)

    xp, pool_p = _pool_mixer(xp, jnp.zeros((bp,) + state_pool.shape[1:], F32), norm_mix[2], pl_w, pl_scale, 0)
    xs, pool_s = _pool_mixer(xs, state_pool, norm_mix[2], pl_w, pl_scale, past_len)
    xp, st = ffn(2, xp, ffn_zero)
    ffn_p.append(st)
    xs, st = ffn(2, xs, state_ffn[2])
    ffn_s.append(st)

    sw = (norm_mix[3], sg_w_in, sg_b_in, sg_ln_g, sg_ln_b, sg_w_s, sg_b_s, sg_w_out)
    assert seq % SG_CHUNK == 0 and past_len % SG_CHUNK == 0 and SG_CHUNK % dec_seq == 0
    (xp2,) = _sg_mixer(xp.reshape(bp * seq, d), *sw, chunk=SG_CHUNK, emit_v=False, tm=2 * LANES)
    xs2, sg_v = _sg_mixer(xs.reshape(bs * dec_seq, d), *sw, chunk=dec_seq, emit_v=True, tm=2 * LANES)
    y_prompt, st = ffn(3, xp2.reshape(bp, seq, d), ffn_zero, final_g=norm_final)
    ffn_p.append(st)
    y_sample, st = ffn(3, xs2.reshape(bs, dec_seq, d), state_ffn[3], final_g=norm_final)
    ffn_s.append(st)

    return (y_prompt, y_sample, conv_p, conv_s,
            k_rows_p.reshape((bp, seq) + kv_shape), v_rows_p.reshape((bp, seq) + kv_shape),
            k_rows_s.reshape((bs, dec_seq) + kv_shape), v_rows_s.reshape((bs, dec_seq) + kv_shape),
            pool_p, pool_s, sg_v.reshape(bs, dec_seq, -1), jnp.stack(ffn_p, axis=0), jnp.stack(ffn_s, axis=0))
```

```python
import functools
import math

import jax
import jax.numpy as jnp
from jax import lax
from jax.experimental import pallas as pl
from jax.experimental.pallas import tpu as pltpu

F32 = jnp.float32
BF16 = jnp.bfloat16
NORM_EPS = 1e-6
POOL_WINDOWS = (2, 4, 8, 16)
SG_CHUNK = 128
SG_GROUPS = 4
DA_HEADS = 8

SUBLANES = 8
LANES = 128
MXU_DIM = 256
VMEM_LIMIT_BYTES = 56 * 1024 * 1024
ROWS_PER_STEP = 512
ATTN_TQ = 512
ATTN_TK = 512


def _round_up(n, m):
    return -(-n // m) * m


def _plan_bt(batch, seq, short_rows):
    if seq >= ROWS_PER_STEP:
        assert seq % ROWS_PER_STEP == 0
        return 1, ROWS_PER_STEP
    bb = max(1, min(batch, short_rows // seq))
    assert batch % bb == 0 and seq % SUBLANES == 0
    return bb, seq


def _const_spec(shape):
    nd = len(shape)
    return pl.BlockSpec(shape, lambda *_: (0,) * nd, pipeline_mode=pl.Buffered(1))


def _params(n_grid):
    return pltpu.CompilerParams(dimension_semantics=("arbitrary",) * n_grid,
                                vmem_limit_bytes=VMEM_LIMIT_BYTES)


def _rms_rows(x, g):
    return x * lax.rsqrt(jnp.mean(x * x, axis=-1, keepdims=True) + NORM_EPS) * g


def _layernorm_rows(x, g, b):
    mu = jnp.mean(x, axis=-1, keepdims=True)
    xc = x - mu
    return xc * lax.rsqrt(jnp.mean(xc * xc, axis=-1, keepdims=True) + NORM_EPS) * g + b


def _sigmoid(x):
    return 1.0 / (1.0 + jnp.exp(-x))


def _silu(x):
    return x * _sigmoid(x)


def _gelu_tanh(x):
    return x * (0.5 * (1.0 + jnp.tanh(math.sqrt(2.0 / math.pi) * (x + 0.044715 * (x * x * x)))))


def _idiv(x, n):
    assert n & (n - 1) == 0
    return x >> (n.bit_length() - 1)


def _imod(x, n):
    assert n & (n - 1) == 0
    return x & (n - 1)


def _dot(a, b):
    return jnp.dot(a, b, preferred_element_type=F32)


def _diff_lambda(lq1_ref, lk1_ref, lq2_ref, lk2_ref, lam_init):
    a = jnp.sum(lq1_ref[...] * lk1_ref[...], axis=-1, keepdims=True)
    b = jnp.sum(lq2_ref[...] * lk2_ref[...], axis=-1, keepdims=True)
    return jnp.exp(a) - jnp.exp(b) + lam_init


def _conv_mixer_kernel(x_ref, st_ref, g_ref, win_ref, bin_ref, wdw_ref, bdw_ref, lng_ref, lnb_ref,
                       wout_ref, bout_ref, o_ref, sto_ref, gp_ref, c_ref, sh_ref, *, bb, tt, kw, hp, sb, rb, cb):
    hist = kw - 1
    t = pl.program_id(1)
    d = x_ref.shape[-1]
    c = gp_ref.shape[-1]
    rows = bb * tt

    @pl.when(t == 0)
    def _():
        gp_ref[:, hp - hist:hp, :] = st_ref[...]

    x = x_ref[...].reshape(rows, d)
    h = _rms_rows(x, g_ref[...]).astype(BF16)
    ag = _dot(h, win_ref[...]) + bin_ref[...]
    glu = ag[:, :c] * _sigmoid(ag[:, c:])
    gp_ref[:, hp:hp + tt, :] = glu.reshape(bb, tt, c)

    base = hp - hist
    span = sh_ref.shape[2]
    for c0 in range(0, c, cb):
        for r in range(1, SUBLANES):
            sh_ref[r - 1] = gp_ref[:, r:r + span, c0:c0 + cb]
        for b0 in range(0, bb, sb):
            for r0 in range(0, tt, rb):
                acc = jnp.broadcast_to(bdw_ref[:, c0:c0 + cb].reshape(1, 1, cb), (sb, rb, cb))
                for k in range(kw):
                    a, r = divmod(base + k, SUBLANES)
                    lo = r0 + SUBLANES * a
                    if r == 0:
                        win = gp_ref[b0:b0 + sb, lo:lo + rb, c0:c0 + cb]
                    else:
                        win = sh_ref[r - 1, b0:b0 + sb, lo:lo + rb, :]
                    acc = acc + win * wdw_ref[k:k + 1, c0:c0 + cb].reshape(1, 1, cb)
                c_ref[b0:b0 + sb, r0:r0 + rb, c0:c0 + cb] = acc

    conv = c_ref[...].reshape(rows, c)
    act = _silu(_layernorm_rows(conv, lng_ref[...], lnb_ref[...])).astype(BF16)
    m = _dot(act, wout_ref[...]) + bout_ref[...]
    o_ref[...] = (x + m).reshape(bb, tt, d)

    new_hist = gp_ref[:, tt + hp - hist:tt + hp, :]
    gp_ref[:, hp - hist:hp, :] = new_hist

    @pl.when(t == pl.num_programs(1) - 1)
    def _():
        sto_ref[...] = new_hist


def _conv_mixer(x, state, g, w_in, b_in, w_dw, b_dw, ln_g, ln_b, w_out, b_out):
    batch, seq, d = x.shape
    kw, c = w_dw.shape
    hist = kw - 1
    hp = _round_up(hist, SUBLANES)
    bb, tt = _plan_bt(batch, seq, short_rows=ROWS_PER_STEP // 4)
    cb = 2 * LANES
    rb = min(tt, LANES)
    sb = max(1, min(bb, LANES // rb))
    kern = functools.partial(_conv_mixer_kernel, bb=bb, tt=tt, kw=kw, hp=hp, sb=sb, rb=rb, cb=cb)
    x_spec = pl.BlockSpec((bb, tt, d), lambda b, t: (b, t, 0))
    st_spec = pl.BlockSpec((bb, hist, c), lambda b, t: (b, 0, 0))
    return pl.pallas_call(
        kern,
        grid=(batch // bb, seq // tt),
        in_specs=[x_spec, st_spec, _const_spec((1, d)), _const_spec((d, 2 * c)), _const_spec((1, 2 * c)),
                  _const_spec((kw, c)), _const_spec((1, c)), _const_spec((1, c)), _const_spec((1, c)),
                  _const_spec((c, d)), _const_spec((1, d))],
        out_specs=[x_spec, st_spec],
        out_shape=[jax.ShapeDtypeStruct((batch, seq, d), F32), jax.ShapeDtypeStruct((batch, hist, c), F32)],
        scratch_shapes=[pltpu.VMEM((bb, hp + tt, c), F32), pltpu.VMEM((bb, tt, c), F32),
                        pltpu.VMEM((SUBLANES - 1, bb, hp + tt - SUBLANES, cb), F32)],
        compiler_params=_params(2),
        name="conv_mixer",
    )(x, state, g.reshape(1, d), w_in.astype(BF16), b_in.reshape(1, 2 * c), w_dw, b_dw.reshape(1, c),
      ln_g.reshape(1, c), ln_b.reshape(1, c), w_out.astype(BF16), b_out.reshape(1, d))


def _ffn_kernel(*refs, bb, tt, proj, final, chunks, hp):
    refs = list(refs)
    x_ref = refs.pop(0)
    if proj:
        m_ref = refs.pop(0)
        wp_ref = refs.pop(0)
    st_ref, g_ref, wg_ref, wu_ref, wdw_ref, bdw_ref, wd_ref = refs[:7]
    refs = refs[7:]
    if final:
        gf_ref = refs.pop(0)
    o_ref, sto_ref, gpad_ref, carry_ref = refs
    hist = wdw_ref.shape[0] - 1
    t = pl.program_id(1)
    d = x_ref.shape[-1]
    rows = bb * tt

    @pl.when(t == 0)
    def _():
        carry_ref[...] = st_ref[...]

    x = x_ref[...].reshape(rows, d)
    if proj:
        x = x + _dot(m_ref[...].reshape(rows, m_ref.shape[-1]).astype(BF16), wp_ref[...])
    h = _rms_rows(x, g_ref[...]).astype(BF16)
    acc = x
    base = hp - hist
    for c0, fc in chunks:
        gate = _dot(h, wg_ref[:, c0:c0 + fc]).reshape(bb, tt, fc)
        up = _dot(h, wu_ref[:, c0:c0 + fc])
        gpad_ref[:, base:hp, :fc] = carry_ref[:, :, c0:c0 + fc]
        gpad_ref[:, hp:hp + tt, :fc] = gate
        gc = gate * wdw_ref[hist:hist + 1, c0:c0 + fc].reshape(1, 1, fc) + bdw_ref[:, c0:c0 + fc].reshape(1, 1, fc)
        for k in range(hist):
            gc = gc + gpad_ref[:, base + k:base + k + tt, :fc] * wdw_ref[k:k + 1, c0:c0 + fc].reshape(1, 1, fc)
        carry_ref[:, :, c0:c0 + fc] = gpad_ref[:, tt + base:tt + hp, :fc]
        act = (_silu(gc).reshape(rows, fc) * up).astype(BF16)
        acc = acc + _dot(act, wd_ref[c0:c0 + fc, :])
    if final:
        acc = _rms_rows(acc, gf_ref[...])
    o_ref[...] = acc.reshape(bb, tt, d)
    sto_ref[...] = carry_ref[...]


def _ffn(x, state, g, w_gate, w_up, w_dw, b_dw, w_down, layer, proj=None, final_g=None):
    batch, seq, d = x.shape
    kw, f = w_dw.shape

    def layer_spec(shape):
        return pl.BlockSpec((None,) + shape, lambda *_: (layer, 0, 0), pipeline_mode=pl.Buffered(1))

    hist = kw - 1
    hp = _round_up(hist, SUBLANES)
    bb, tt = _plan_bt(batch, seq, short_rows=ROWS_PER_STEP // 2)
    n_tiles = f // MXU_DIM
    assert f % MXU_DIM == 0
    half = (n_tiles + 1) // 2 * MXU_DIM
    chunks = ((0, half), (half, f - half)) if f > half else ((0, f),)
    kern = functools.partial(_ffn_kernel, bb=bb, tt=tt, proj=proj is not None, final=final_g is not None,
                             chunks=chunks, hp=hp)
    x_spec = pl.BlockSpec((bb, tt, d), lambda b, t: (b, t, 0))
    st_spec = pl.BlockSpec((bb, hist, f), lambda b, t: (b, 0, 0))
    args, specs = [x], [x_spec]
    if proj is not None:
        m, w_proj = proj
        dm = m.shape[-1]
        args += [m, w_proj.astype(BF16)]
        specs += [pl.BlockSpec((bb, tt, dm), lambda b, t: (b, t, 0)), _const_spec((dm, d))]
    args += [state, g.reshape(1, d), w_gate, w_up, w_dw, b_dw.reshape(1, f), w_down]
    specs += [st_spec, _const_spec((1, d)), layer_spec((d, f)), layer_spec((d, f)), _const_spec((kw, f)),
              _const_spec((1, f)), layer_spec((f, d))]
    if final_g is not None:
        args.append(final_g.reshape(1, d))
        specs.append(_const_spec((1, d)))
    return pl.pallas_call(
        kern,
        grid=(batch // bb, seq // tt),
        in_specs=specs,
        out_specs=[x_spec, st_spec],
        out_shape=[jax.ShapeDtypeStruct((batch, seq, d), F32), jax.ShapeDtypeStruct((batch, hist, f), F32)],
        scratch_shapes=[pltpu.VMEM((bb, hp + tt, chunks[0][1]), F32), pltpu.VMEM((bb, hist, f), F32)],
        compiler_params=_params(2),
        name="conv_ffn",
    )(*args)


def _qkv_kernel(x_ref, g_ref, w_ref, q_ref, k_ref, v_ref, *maybe_bf, scale):
    dq = q_ref.shape[-1]
    h = _rms_rows(x_ref[...], g_ref[...]).astype(BF16)
    qkv = _dot(h, w_ref[...])
    q_ref[...] = (qkv[:, :dq] * scale).astype(q_ref.dtype)
    k = qkv[:, dq:2 * dq]
    v = qkv[:, 2 * dq:]
    k_ref[...] = k
    v_ref[...] = v
    if maybe_bf:
        kb_ref, vt_ref = maybe_bf
        kb_ref[...] = k.astype(BF16)
        vt_ref[...] = v.T.astype(BF16)


def _qkv(x2d, g, w_qkv, scale, q_dtype, seq=None):
    n, d = x2d.shape
    dq = w_qkv.shape[1] // 3
    tm = min(n, ROWS_PER_STEP)
    row_spec = pl.BlockSpec((tm, dq), lambda i: (i, 0))
    out_shape = [jax.ShapeDtypeStruct((n, dq), q_dtype), jax.ShapeDtypeStruct((n, dq), F32),
                 jax.ShapeDtypeStruct((n, dq), F32)]
    out_specs = [row_spec] * 3
    if seq is not None:
        assert seq % tm == 0
        per_seq = seq // tm
        out_shape += [jax.ShapeDtypeStruct((n, dq), BF16), jax.ShapeDtypeStruct((n // seq, dq, seq), BF16)]
        out_specs += [row_spec, pl.BlockSpec((None, dq, tm), lambda i: (i // per_seq, 0, i % per_seq))]
    return pl.pallas_call(
        functools.partial(_qkv_kernel, scale=scale),
        grid=(n // tm,),
        in_specs=[pl.BlockSpec((tm, d), lambda i: (i, 0)), _const_spec((1, d)), _const_spec((d, 3 * dq))],
        out_specs=out_specs,
        out_shape=out_shape,
        compiler_params=_params(1),
        name="qkv",
    )(x2d, g.reshape(1, d), w_qkv.astype(BF16))


def _subln(o, ng, lam_init):
    return o * lax.rsqrt(jnp.mean(o * o, axis=-1, keepdims=True) + NORM_EPS) * ng * (1.0 - lam_init)


def _attn_prompt_kernel(lq1_ref, lk1_ref, lq2_ref, lk2_ref, ng_ref, q_ref, k_ref, vt_ref, o_ref, sa_ref, sb_ref,
                        *, tq, tk, lam_init):
    qi = pl.program_id(2)
    e = q_ref.shape[-1]
    q = q_ref[...].astype(F32)
    lane = lax.broadcasted_iota(jnp.int32, q.shape, 1)
    qq = jnp.concatenate([jnp.where(lane < e // 2, q, 0.0), jnp.where(lane >= e // 2, q, 0.0)], axis=0)
    qqt = qq.T.astype(BF16)

    def scores(c, dst_ref):
        start = pl.multiple_of(c * tk, tk)
        dst_ref[...] = _dot(k_ref[pl.ds(start, tk), :], qqt)

    def absorb(src_ref, c, carry, masked):
        m, l, acc = carry
        start = pl.multiple_of(c * tk, tk)
        s = src_ref[...]
        if masked:
            key = lax.broadcasted_iota(jnp.int32, s.shape, 0) + start
            col = lax.broadcasted_iota(jnp.int32, s.shape, 1)
            s = jnp.where(key <= jnp.where(col >= tq, col - tq, col) + qi * tq, s, -jnp.inf)
        m_new = jnp.maximum(m, jnp.max(s, axis=0, keepdims=True))
        p = jnp.exp2(s - m_new)
        alpha = jnp.exp2(m - m_new)
        l = alpha * l + jnp.sum(p, axis=0, keepdims=True)
        return m_new, l, alpha * acc + _dot(vt_ref[:, pl.ds(start, tk)], p.astype(BF16))

    init = (jnp.full((1, 2 * tq), -jnp.inf, F32), jnp.zeros((1, 2 * tq), F32), jnp.zeros((e, 2 * tq), F32))
    n = (qi * tq) // tk
    scores(0, sa_ref)

    def two_chunks(i, carry):
        scores(2 * i + 1, sb_ref)
        carry = absorb(sa_ref, 2 * i, carry, False)
        scores(2 * i + 2, sa_ref)
        return absorb(sb_ref, 2 * i + 1, carry, False)

    carry = lax.fori_loop(0, n // 2, two_chunks, init)

    def tail_even(carry):
        return absorb(sa_ref, n, carry, True)

    def tail_odd(carry):
        scores(n, sb_ref)
        carry = absorb(sa_ref, n - 1, carry, False)
        return absorb(sb_ref, n, carry, True)

    _, l, acc = lax.cond(n % 2 == 0, tail_even, tail_odd, carry)
    o = acc * (1.0 / l)
    lam = _diff_lambda(lq1_ref, lk1_ref, lq2_ref, lk2_ref, lam_init)
    o = (o[:, :tq] - lam * o[:, tq:]).T
    o_ref[...] = _subln(o, ng_ref[...], lam_init).astype(o_ref.dtype)


def _attn_prompt(q, k, vt, lqk, norm_g, lam_init, tq, tk):
    batch, seq, dq = q.shape
    e = dq // DA_HEADS
    dl = lqk[0].shape[0]
    assert seq % tk == 0 and tk % tq == 0
    q_spec = pl.BlockSpec((None, tq, e), lambda b, h, i: (b, i, h))
    k_spec = pl.BlockSpec((None, seq, e), lambda b, h, i: (b, 0, h))
    vt_spec = pl.BlockSpec((None, e, seq), lambda b, h, i: (b, h, 0))
    return pl.pallas_call(
        functools.partial(_attn_prompt_kernel, tq=tq, tk=tk, lam_init=lam_init),
        grid=(batch, DA_HEADS, seq // tq),
        in_specs=[_const_spec((1, dl))] * 4 + [pl.BlockSpec((1, e), lambda b, h, i: (0, h)), q_spec, k_spec, vt_spec],
        out_specs=q_spec,
        out_shape=jax.ShapeDtypeStruct((batch, seq, dq), BF16),
        scratch_shapes=[pltpu.VMEM((tk, 2 * tq), F32), pltpu.VMEM((tk, 2 * tq), F32)],
        compiler_params=_params(3),
        name="attn_prompt",
    )(*[a.reshape(1, dl) for a in lqk], norm_g.reshape(1, dq), q, k, vt)


def _attn_sample_kernel(pt_ref, lq1_ref, lk1_ref, lq2_ref, lk2_ref, ng_ref, q_ref, kn_ref, vn_ref, *rest,
                        n_pages, lam_init):
    del pt_ref
    k_pages = rest[:n_pages]
    v_pages = rest[n_pages:2 * n_pages]
    o_ref, s_ref = rest[2 * n_pages:]
    t, nh, e = q_ref.shape
    page = k_pages[0].shape[0]
    rows = t * nh
    cols = page * nh
    nt = (((1,), (1,)), ((), ()))

    q2 = q_ref[...].reshape(rows, e)
    lane = lax.broadcasted_iota(jnp.int32, q2.shape, 1)
    wq = jnp.concatenate([jnp.where(lane < e // 2, q2, 0.0), jnp.where(lane >= e // 2, q2, 0.0)], axis=0).astype(BF16)

    r = lax.broadcasted_iota(jnp.int32, (2 * rows, cols), 0)
    c = lax.broadcasted_iota(jnp.int32, (2 * rows, cols), 1)
    same_head = _imod(c - r, nh) == 0

    rn = lax.broadcasted_iota(jnp.int32, (2 * rows, rows), 0)
    cn = lax.broadcasted_iota(jnp.int32, (2 * rows, rows), 1)
    keep_new = (_imod(cn - rn, nh) == 0) & (_idiv(cn, nh) <= _idiv(_imod(rn, rows), nh))
    s_new = lax.dot_general(wq, kn_ref[...].reshape(rows, e).astype(BF16), nt, preferred_element_type=F32)
    s_new = jnp.where(keep_new, s_new, -jnp.inf)

    mx = None
    for p in range(n_pages):
        k2 = k_pages[p][...].reshape(cols, e).astype(BF16)
        s = lax.dot_general(wq, k2, nt, preferred_element_type=F32)
        s_ref[:, p * cols:(p + 1) * cols] = s
        mx = s if mx is None else jnp.maximum(mx, s)
    mx = jnp.where(same_head, mx, -jnp.inf)
    m = jnp.maximum(jnp.max(mx, axis=-1, keepdims=True), jnp.max(s_new, axis=-1, keepdims=True))

    shift = jnp.where(same_head, -m, -jnp.inf)
    p_new = jnp.exp2(s_new - m)
    acc = _dot(p_new.astype(BF16), vn_ref[...].reshape(rows, e).astype(BF16))
    tot = None
    for p in range(n_pages):
        pe = jnp.exp2(s_ref[:, p * cols:(p + 1) * cols] + shift)
        tot = pe if tot is None else tot + pe
        acc = acc + _dot(pe.astype(BF16), v_pages[p][...].reshape(cols, e).astype(BF16))
    l = jnp.sum(tot, axis=-1, keepdims=True) + jnp.sum(p_new, axis=-1, keepdims=True)

    o = acc * (1.0 / l)
    lam = _diff_lambda(lq1_ref, lk1_ref, lq2_ref, lk2_ref, lam_init)
    o = o[:rows] - lam * o[rows:]
    ng = jnp.concatenate([ng_ref[...]] * t, axis=0)
    o_ref[...] = _subln(o, ng, lam_init).reshape(t, nh, e)


def _attn_sample(q, k_new, v_new, cache_k, cache_v, page_table, lqk, norm_g, lam_init):
    batch, t, nh, e = q.shape
    page = cache_k.shape[1]
    n_pages = page_table.shape[1]
    dl = lqk[0].shape[0]
    row_spec = pl.BlockSpec((None, t, nh, e), lambda b, pt: (b, 0, 0, 0))

    def page_spec(p):
        return pl.BlockSpec((None, page, nh, e), lambda b, pt: (pt[b, p], 0, 0, 0))

    page_specs = [page_spec(p) for p in range(n_pages)]
    grid_spec = pltpu.PrefetchScalarGridSpec(
        num_scalar_prefetch=1,
        grid=(batch,),
        in_specs=[_const_spec((1, dl))] * 4 + [_const_spec((nh, e)), row_spec, row_spec, row_spec]
        + page_specs + page_specs,
        out_specs=row_spec,
        scratch_shapes=[pltpu.VMEM((2 * t * nh, n_pages * page * nh), F32)],
    )
    return pl.pallas_call(
        functools.partial(_attn_sample_kernel, n_pages=n_pages, lam_init=lam_init),
        grid_spec=grid_spec,
        out_shape=jax.ShapeDtypeStruct((batch, t, nh, e), F32),
        compiler_params=_params(1),
        name="attn_sample",
    )(page_table, *[a.reshape(1, dl) for a in lqk], norm_g.reshape(nh, e), q, k_new, v_new,
      *([cache_k] * n_pages), *([cache_v] * n_pages))


def _pool_kernel(x_ref, st_ref, g_ref, w_ref, sc_ref, o_ref, sto_ref, hp_ref, *, bb, tt, windows, pos0, hist, hp):
    t = pl.program_id(1)
    d = x_ref.shape[-1]
    cg = d // len(windows)
    rows = bb * tt

    @pl.when(t == 0)
    def _():
        hp_ref[:, hp - hist:hp, :] = st_ref[...]

    x = x_ref[...].reshape(rows, d)
    h = _rms_rows(x, g_ref[...]).reshape(bb, tt, d)
    hp_ref[:, hp:hp + tt, :] = h
    pos = pos0 + t * tt + lax.broadcasted_iota(jnp.int32, (1, tt, 1), 1)
    ys = []
    for gi, win in enumerate(windows):
        c0 = gi * cg
        cur = h[:, :, c0:c0 + cg]
        tot = cur
        for j in range(1, win):
            tot = tot + hp_ref[:, hp - j:hp - j + tt, c0:c0 + cg]
        cnt = jnp.minimum(win, pos + 1).astype(F32)
        pooled = (tot / cnt - cur).reshape(rows, cg).astype(BF16)
        ys.append(_dot(pooled, w_ref[gi]))
    y = jnp.concatenate(ys, axis=-1) * sc_ref[...]
    o_ref[...] = (x + y).reshape(bb, tt, d)

    new_hist = hp_ref[:, tt + hp - hist:tt + hp, :]
    hp_ref[:, hp - hist:hp, :] = new_hist

    @pl.when(t == pl.num_programs(1) - 1)
    def _():
        sto_ref[...] = new_hist


def _pool_mixer(x, state, g, w_grp, scale, pos0):
    batch, seq, d = x.shape
    hist = max(POOL_WINDOWS) - 1
    hp = _round_up(hist, SUBLANES)
    n_grp, cg, _ = w_grp.shape
    bb, tt = _plan_bt(batch, seq, short_rows=ROWS_PER_STEP // 2)
    x_spec = pl.BlockSpec((bb, tt, d), lambda b, t: (b, t, 0))
    st_spec = pl.BlockSpec((bb, hist, d), lambda b, t: (b, 0, 0))
    return pl.pallas_call(
        functools.partial(_pool_kernel, bb=bb, tt=tt, windows=POOL_WINDOWS, pos0=pos0, hist=hist, hp=hp),
        grid=(batch // bb, seq // tt),
        in_specs=[x_spec, st_spec, _const_spec((1, d)), _const_spec((n_grp, cg, cg)), _const_spec((1, d))],
        out_specs=[x_spec, st_spec],
        out_shape=[jax.ShapeDtypeStruct((batch, seq, d), F32), jax.ShapeDtypeStruct((batch, hist, d), F32)],
        scratch_shapes=[pltpu.VMEM((bb, hp + tt, d), F32)],
        compiler_params=_params(2),
        name="pool_mixer",
    )(x, state, g.reshape(1, d), w_grp.astype(BF16), scale.reshape(1, d))


def _sg_kernel(x_ref, g_ref, win_ref, bin_ref, lng_ref, lnb_ref, ws_ref, bs_ref, wout_ref, o_ref, *maybe_v,
               tm, chunk):
    sg = wout_ref.shape[0]
    cg = sg // SG_GROUPS
    x = x_ref[...]
    h = _rms_rows(x, g_ref[...]).astype(BF16)
    z = _gelu_tanh(_dot(h, win_ref[...]) + bin_ref[...])
    u = z[:, :sg]
    v = _layernorm_rows(z[:, sg:], lng_ref[...], lnb_ref[...])
    if maybe_v:
        maybe_v[0][...] = v
    vb = v.astype(BF16)
    r = lax.broadcasted_iota(jnp.int32, (SG_CHUNK, SG_CHUNK), 0)
    c = lax.broadcasted_iota(jnp.int32, (SG_CHUNK, SG_CHUNK), 1)
    keep = (_idiv(r, chunk) == _idiv(c, chunk)) & (c <= r)
    acc = x
    for gi in range(SG_GROUPS):
        ws = jnp.where(keep, ws_ref[gi], 0.0).astype(BF16)
        parts = []
        for r0 in range(0, tm, SG_CHUNK):
            parts.append(_dot(ws, vb[r0:r0 + SG_CHUNK, gi * cg:(gi + 1) * cg]) + bs_ref[gi])
        s = jnp.concatenate(parts, axis=0) if len(parts) > 1 else parts[0]
        gated = (u[:, gi * cg:(gi + 1) * cg] * s).astype(BF16)
        acc = acc + _dot(gated, wout_ref[gi * cg:(gi + 1) * cg, :])
    o_ref[...] = acc


def _sg_mixer(x2d, g, w_in, b_in, ln_g, ln_b, w_s, b_s, w_out, chunk, emit_v, tm):
    n, d = x2d.shape
    sg = w_out.shape[0]
    reps = SG_CHUNK // chunk
    ws = jnp.tile(w_s[:, :chunk, :chunk], (1, reps, reps))
    bs = jnp.tile(b_s[:, :chunk], (1, reps)).reshape(SG_GROUPS, SG_CHUNK, 1)
    row_spec = pl.BlockSpec((tm, d), lambda i: (i, 0))
    out_shape = [jax.ShapeDtypeStruct((n, d), F32)]
    out_specs = [row_spec]
    if emit_v:
        out_shape.append(jax.ShapeDtypeStruct((n, sg), F32))
        out_specs.append(pl.BlockSpec((tm, sg), lambda i: (i, 0)))
    return pl.pallas_call(
        functools.partial(_sg_kernel, tm=tm, chunk=chunk),
        grid=(n // tm,),
        in_specs=[row_spec, _const_spec((1, d)), _const_spec((d, 2 * sg)), _const_spec((1, 2 * sg)),
                  _const_spec((1, sg)), _const_spec((1, sg)), _const_spec((SG_GROUPS, SG_CHUNK, SG_CHUNK)),
                  _const_spec((SG_GROUPS, SG_CHUNK, 1)), _const_spec((sg, d))],
        out_specs=out_specs,
        out_shape=out_shape,
        compiler_params=_params(1),
        name="sg_mixer",
    )(x2d, g.reshape(1, d), w_in.astype(BF16), b_in.reshape(1, 2 * sg), ln_g.reshape(1, sg), ln_b.reshape(1, sg),
      ws, bs, w_out.astype(BF16))


def kernel(x_prompt, x_sample, state_conv, cache_k, cache_v, page_table, state_pool, state_ffn, norm_mix, norm_ffn, norm_final, cv_w_in, cv_b_in, cv_w_dw, cv_b_dw, cv_ln_g, cv_ln_b, cv_w_out, cv_b_out, da_w_qkv, da_lq1, da_lk1, da_lq2, da_lk2, da_norm_g, da_w_o, pl_w, pl_scale, sg_w_in, sg_b_in, sg_ln_g, sg_ln_b, sg_w_s, sg_b_s, sg_w_out, ff_w_gate, ff_w_up, ff_w_dw, ff_b_dw, ff_w_down):
    bp, seq, d = x_prompt.shape
    bs, dec_seq, _ = x_sample.shape
    depth, ffn_kw, d_ff = ff_w_dw.shape
    past_len = page_table.shape[1] * cache_k.shape[1]
    head_dim = d // (2 * DA_HEADS)
    kv_shape = (DA_HEADS, 2 * head_dim)

    wg_all, wu_all, wd_all = ff_w_gate.astype(BF16), ff_w_up.astype(BF16), ff_w_down.astype(BF16)

    def ffn(i, x, state, **kw):
        return _ffn(x, state, norm_ffn[i], wg_all, wu_all, ff_w_dw[i], ff_b_dw[i], wd_all, i, **kw)

    ffn_zero = jnp.zeros((bp, ffn_kw - 1, d_ff), F32)
    ffn_p, ffn_s = [], []

    cw = (norm_mix[0], cv_w_in, cv_b_in, cv_w_dw, cv_b_dw, cv_ln_g, cv_ln_b, cv_w_out, cv_b_out)
    xp, conv_p = _conv_mixer(x_prompt, jnp.zeros((bp,) + state_conv.shape[1:], F32), *cw)
    xs, conv_s = _conv_mixer(x_sample, state_conv, *cw)
    xp, st = ffn(0, xp, ffn_zero)
    ffn_p.append(st)
    xs, st = ffn(0, xs, state_ffn[0])
    ffn_s.append(st)

    lam_init = 0.8 - 0.6 * math.exp(-0.3 * 1)
    lqk = (da_lq1, da_lk1, da_lq2, da_lk2)
    scale = head_dim ** -0.5 * math.log2(math.e)
    q_p, k_rows_p, v_rows_p, kb_p, vt_p = _qkv(xp.reshape(bp * seq, d), norm_mix[1], da_w_qkv, scale, BF16, seq=seq)
    o_p = _attn_prompt(q_p.reshape(bp, seq, d), kb_p.reshape(bp, seq, d), vt_p, lqk,
                       da_norm_g, lam_init, tq=min(seq, ATTN_TQ), tk=min(seq, ATTN_TK))
    xp, st = ffn(1, xp, ffn_zero, proj=(o_p, da_w_o))
    ffn_p.append(st)
    q_s, k_rows_s, v_rows_s = _qkv(xs.reshape(bs * dec_seq, d), norm_mix[1], da_w_qkv, scale, F32)
    rows_s = (bs, dec_seq) + kv_shape
    o_s = _attn_sample(q_s.reshape(rows_s), k_rows_s.reshape(rows_s), v_rows_s.reshape(rows_s),
                       cache_k, cache_v, page_table, lqk, da_norm_g, lam_init)
    xs, st = ffn(1, xs, state_ffn[1], proj=(o_s.reshape(bs, dec_seq, d), da_w_o))
    ffn_s.append(st)

    xp, pool_p = _pool_mixer(xp, jnp.zeros((bp,) + state_pool.shape[1:], F32), norm_mix[2], pl_w, pl_scale, 0)
    xs, pool_s = _pool_mixer(xs, state_pool, norm_mix[2], pl_w, pl_scale, past_len)
    xp, st = ffn(2, xp, ffn_zero)
    ffn_p.append(st)
    xs, st = ffn(2, xs, state_ffn[2])
    ffn_s.append(st)

    sw = (norm_mix[3], sg_w_in, sg_b_in, sg_ln_g, sg_ln_b, sg_w_s, sg_b_s, sg_w_out)
    assert seq % SG_CHUNK == 0 and past_len % SG_CHUNK == 0 and SG_CHUNK % dec_seq == 0
    (xp2,) = _sg_mixer(xp.reshape(bp * seq, d), *sw, chunk=SG_CHUNK, emit_v=False, tm=2 * LANES)
    xs2, sg_v = _sg_mixer(xs.reshape(bs * dec_seq, d), *sw, chunk=dec_seq, emit_v=True, tm=2 * LANES)
    y_prompt, st = ffn(3, xp2.reshape(bp, seq, d), ffn_zero, final_g=norm_final)
    ffn_p.append(st)
    y_sample, st = ffn(3, xs2.reshape(bs, dec_seq, d), state_ffn[3], final_g=norm_final)
    ffn_s.append(st)

    return (y_prompt, y_sample, conv_p, conv_s,
            k_rows_p.reshape((bp, seq) + kv_shape), v_rows_p.reshape((bp, seq) + kv_shape),
            k_rows_s.reshape((bs, dec_seq) + kv_shape), v_rows_s.reshape((bs, dec_seq) + kv_shape),
            pool_p, pool_s, sg_v.reshape(bs, dec_seq, -1), jnp.stack(ffn_p, axis=0), jnp.stack(ffn_s, axis=0))
```

```python
import functools
import math

import jax
import jax.numpy as jnp
from jax import lax
from jax.experimental import pallas as pl
from jax.experimental.pallas import tpu as pltpu

F32 = jnp.float32
BF16 = jnp.bfloat16
NORM_EPS = 1e-6
POOL_WINDOWS = (2, 4, 8, 16)
SG_CHUNK = 128
SG_GROUPS = 4
DA_HEADS = 8

SUBLANES = 8
LANES = 128
MXU_DIM = 256
VMEM_LIMIT_BYTES = 56 * 1024 * 1024
ROWS_PER_STEP = 512
SG_SUB_ROWS = 256
ATTN_TQ = 512
ATTN_TK = 512


def _round_up(n, m):
    return -(-n // m) * m


def _plan_bt(batch, seq, short_rows):
    if seq >= ROWS_PER_STEP:
        assert seq % ROWS_PER_STEP == 0
        return 1, ROWS_PER_STEP
    bb = max(1, min(batch, short_rows // seq))
    assert batch % bb == 0 and seq % SUBLANES == 0
    return bb, seq


def _const_spec(shape):
    nd = len(shape)
    return pl.BlockSpec(shape, lambda *_: (0,) * nd, pipeline_mode=pl.Buffered(1))


def _params(n_grid):
    return pltpu.CompilerParams(dimension_semantics=("arbitrary",) * n_grid,
                                vmem_limit_bytes=VMEM_LIMIT_BYTES)


def _rms_rows(x, g):
    return x * lax.rsqrt(jnp.mean(x * x, axis=-1, keepdims=True) + NORM_EPS) * g


def _layernorm_rows(x, g, b):
    mu = jnp.mean(x, axis=-1, keepdims=True)
    xc = x - mu
    return xc * lax.rsqrt(jnp.mean(xc * xc, axis=-1, keepdims=True) + NORM_EPS) * g + b


def _sigmoid(x):
    return 1.0 / (1.0 + jnp.exp(-x))


def _silu(x):
    return x * _sigmoid(x)


def _gelu_tanh(x):
    return x * (0.5 * (1.0 + jnp.tanh(math.sqrt(2.0 / math.pi) * (x + 0.044715 * (x * x * x)))))


def _idiv(x, n):
    assert n & (n - 1) == 0
    return x >> (n.bit_length() - 1)


def _imod(x, n):
    assert n & (n - 1) == 0
    return x & (n - 1)


def _dot(a, b):
    return jnp.dot(a, b, preferred_element_type=F32)


def _diff_lambda(lq1_ref, lk1_ref, lq2_ref, lk2_ref, lam_init):
    a = jnp.sum(lq1_ref[...] * lk1_ref[...], axis=-1, keepdims=True)
    b = jnp.sum(lq2_ref[...] * lk2_ref[...], axis=-1, keepdims=True)
    return jnp.exp(a) - jnp.exp(b) + lam_init


def _conv_mixer_kernel(x_ref, st_ref, g_ref, win_ref, bin_ref, wdw_ref, bdw_ref, lng_ref, lnb_ref,
                       wout_ref, bout_ref, o_ref, sto_ref, gp_ref, c_ref, sh_ref, *, bb, tt, kw, hp, sb, rb, cb):
    hist = kw - 1
    t = pl.program_id(1)
    d = x_ref.shape[-1]
    c = gp_ref.shape[-1]
    rows = bb * tt

    @pl.when(t == 0)
    def _():
        gp_ref[:, hp - hist:hp, :] = st_ref[...]

    x = x_ref[...].reshape(rows, d)
    h = _rms_rows(x, g_ref[...]).astype(BF16)
    ag = _dot(h, win_ref[...]) + bin_ref[...]
    glu = ag[:, :c] * _sigmoid(ag[:, c:])
    gp_ref[:, hp:hp + tt, :] = glu.reshape(bb, tt, c)

    base = hp - hist
    span = sh_ref.shape[2]
    for c0 in range(0, c, cb):
        for r in range(1, SUBLANES):
            sh_ref[r - 1] = gp_ref[:, r:r + span, c0:c0 + cb]
        for b0 in range(0, bb, sb):
            for r0 in range(0, tt, rb):
                acc = jnp.broadcast_to(bdw_ref[:, c0:c0 + cb].reshape(1, 1, cb), (sb, rb, cb))
                for k in range(kw):
                    a, r = divmod(base + k, SUBLANES)
                    lo = r0 + SUBLANES * a
                    if r == 0:
                        win = gp_ref[b0:b0 + sb, lo:lo + rb, c0:c0 + cb]
                    else:
                        win = sh_ref[r - 1, b0:b0 + sb, lo:lo + rb, :]
                    acc = acc + win * wdw_ref[k:k + 1, c0:c0 + cb].reshape(1, 1, cb)
                c_ref[b0:b0 + sb, r0:r0 + rb, c0:c0 + cb] = acc

    conv = c_ref[...].reshape(rows, c)
    act = _silu(_layernorm_rows(conv, lng_ref[...], lnb_ref[...])).astype(BF16)
    m = _dot(act, wout_ref[...]) + bout_ref[...]
    o_ref[...] = (x + m).reshape(bb, tt, d)

    new_hist = gp_ref[:, tt + hp - hist:tt + hp, :]
    gp_ref[:, hp - hist:hp, :] = new_hist

    @pl.when(t == pl.num_programs(1) - 1)
    def _():
        sto_ref[...] = new_hist


def _conv_mixer(x, state, g, w_in, b_in, w_dw, b_dw, ln_g, ln_b, w_out, b_out):
    batch, seq, d = x.shape
    kw, c = w_dw.shape
    hist = kw - 1
    hp = _round_up(hist, SUBLANES)
    bb, tt = _plan_bt(batch, seq, short_rows=ROWS_PER_STEP // 4)
    cb = 2 * LANES
    rb = min(tt, LANES)
    sb = max(1, min(bb, LANES // rb))
    kern = functools.partial(_conv_mixer_kernel, bb=bb, tt=tt, kw=kw, hp=hp, sb=sb, rb=rb, cb=cb)
    x_spec = pl.BlockSpec((bb, tt, d), lambda b, t: (b, t, 0))
    st_spec = pl.BlockSpec((bb, hist, c), lambda b, t: (b, 0, 0))
    return pl.pallas_call(
        kern,
        grid=(batch // bb, seq // tt),
        in_specs=[x_spec, st_spec, _const_spec((1, d)), _const_spec((d, 2 * c)), _const_spec((1, 2 * c)),
                  _const_spec((kw, c)), _const_spec((1, c)), _const_spec((1, c)), _const_spec((1, c)),
                  _const_spec((c, d)), _const_spec((1, d))],
        out_specs=[x_spec, st_spec],
        out_shape=[jax.ShapeDtypeStruct((batch, seq, d), F32), jax.ShapeDtypeStruct((batch, hist, c), F32)],
        scratch_shapes=[pltpu.VMEM((bb, hp + tt, c), F32), pltpu.VMEM((bb, tt, c), F32),
                        pltpu.VMEM((SUBLANES - 1, bb, hp + tt - SUBLANES, cb), F32)],
        compiler_params=_params(2),
        name="conv_mixer",
    )(x, state, g.reshape(1, d), w_in.astype(BF16), b_in.reshape(1, 2 * c), w_dw, b_dw.reshape(1, c),
      ln_g.reshape(1, c), ln_b.reshape(1, c), w_out.astype(BF16), b_out.reshape(1, d))


def _ffn_kernel(*refs, bb, tt, proj, final, chunks, hp):
    refs = list(refs)
    x_ref = refs.pop(0)
    if proj:
        m_ref = refs.pop(0)
        wp_ref = refs.pop(0)
    st_ref, g_ref, wg_ref, wu_ref, wdw_ref, bdw_ref, wd_ref = refs[:7]
    refs = refs[7:]
    if final:
        gf_ref = refs.pop(0)
    o_ref, sto_ref, gpad_ref, carry_ref = refs
    hist = wdw_ref.shape[0] - 1
    t = pl.program_id(1)
    d = x_ref.shape[-1]
    rows = bb * tt

    @pl.when(t == 0)
    def _():
        carry_ref[...] = st_ref[...]

    x = x_ref[...].reshape(rows, d)
    if proj:
        x = x + _dot(m_ref[...].reshape(rows, m_ref.shape[-1]).astype(BF16), wp_ref[...])
    h = _rms_rows(x, g_ref[...]).astype(BF16)
    acc = x
    base = hp - hist
    for c0, fc in chunks:
        gate = _dot(h, wg_ref[:, c0:c0 + fc]).reshape(bb, tt, fc)
        up = _dot(h, wu_ref[:, c0:c0 + fc])
        gpad_ref[:, base:hp, :fc] = carry_ref[:, :, c0:c0 + fc]
        gpad_ref[:, hp:hp + tt, :fc] = gate
        gc = gate * wdw_ref[hist:hist + 1, c0:c0 + fc].reshape(1, 1, fc) + bdw_ref[:, c0:c0 + fc].reshape(1, 1, fc)
        for k in range(hist):
            gc = gc + gpad_ref[:, base + k:base + k + tt, :fc] * wdw_ref[k:k + 1, c0:c0 + fc].reshape(1, 1, fc)
        carry_ref[:, :, c0:c0 + fc] = gpad_ref[:, tt + base:tt + hp, :fc]
        act = (_silu(gc).reshape(rows, fc) * up).astype(BF16)
        acc = acc + _dot(act, wd_ref[c0:c0 + fc, :])
    if final:
        acc = _rms_rows(acc, gf_ref[...])
    o_ref[...] = acc.reshape(bb, tt, d)
    sto_ref[...] = carry_ref[...]


def _ffn(x, state, g, w_gate, w_up, w_dw, b_dw, w_down, layer, proj=None, final_g=None):
    batch, seq, d = x.shape
    kw, f = w_dw.shape

    def layer_spec(shape):
        return pl.BlockSpec((None,) + shape, lambda *_: (layer, 0, 0), pipeline_mode=pl.Buffered(1))

    hist = kw - 1
    hp = _round_up(hist, SUBLANES)
    bb, tt = _plan_bt(batch, seq, short_rows=ROWS_PER_STEP // 2)
    n_tiles = f // MXU_DIM
    assert f % MXU_DIM == 0
    half = (n_tiles + 1) // 2 * MXU_DIM
    chunks = ((0, half), (half, f - half)) if f > half else ((0, f),)
    kern = functools.partial(_ffn_kernel, bb=bb, tt=tt, proj=proj is not None, final=final_g is not None,
                             chunks=chunks, hp=hp)
    x_spec = pl.BlockSpec((bb, tt, d), lambda b, t: (b, t, 0))
    st_spec = pl.BlockSpec((bb, hist, f), lambda b, t: (b, 0, 0))
    args, specs = [x], [x_spec]
    if proj is not None:
        m, w_proj = proj
        dm = m.shape[-1]
        args += [m, w_proj.astype(BF16)]
        specs += [pl.BlockSpec((bb, tt, dm), lambda b, t: (b, t, 0)), _const_spec((dm, d))]
    args += [state, g.reshape(1, d), w_gate, w_up, w_dw, b_dw.reshape(1, f), w_down]
    specs += [st_spec, _const_spec((1, d)), layer_spec((d, f)), layer_spec((d, f)), _const_spec((kw, f)),
              _const_spec((1, f)), layer_spec((f, d))]
    if final_g is not None:
        args.append(final_g.reshape(1, d))
        specs.append(_const_spec((1, d)))
    return pl.pallas_call(
        kern,
        grid=(batch // bb, seq // tt),
        in_specs=specs,
        out_specs=[x_spec, st_spec],
        out_shape=[jax.ShapeDtypeStruct((batch, seq, d), F32), jax.ShapeDtypeStruct((batch, hist, f), F32)],
        scratch_shapes=[pltpu.VMEM((bb, hp + tt, chunks[0][1]), F32), pltpu.VMEM((bb, hist, f), F32)],
        compiler_params=_params(2),
        name="conv_ffn",
    )(*args)


def _qkv_kernel(x_ref, g_ref, w_ref, q_ref, k_ref, v_ref, *maybe_heads, scale):
    dq = k_ref.shape[-1]
    h = _rms_rows(x_ref[...], g_ref[...]).astype(BF16)
    qkv = _dot(h, w_ref[...])
    q = qkv[:, :dq] * scale
    k = qkv[:, dq:2 * dq]
    v = qkv[:, 2 * dq:]
    k_ref[...] = k
    v_ref[...] = v
    if maybe_heads:
        kh_ref, vt_ref = maybe_heads
        q_ref[...] = q.T.astype(BF16)
        vt_ref[...] = v.T.astype(BF16)
        e = kh_ref.shape[-1]
        for hd in range(kh_ref.shape[0]):
            kh_ref[hd] = k[:, hd * e:(hd + 1) * e].astype(BF16)
    else:
        q_ref[...] = q


def _qkv(x2d, g, w_qkv, scale, seq=None):
    n, d = x2d.shape
    dq = w_qkv.shape[1] // 3
    tm = min(n, ROWS_PER_STEP)
    row_spec = pl.BlockSpec((tm, dq), lambda i: (i, 0))
    rows_f32 = jax.ShapeDtypeStruct((n, dq), F32)
    if seq is None:
        out_shape = [rows_f32] * 3
        out_specs = [row_spec] * 3
    else:
        assert seq % tm == 0
        per_seq = seq // tm
        e = dq // DA_HEADS
        t_shape = jax.ShapeDtypeStruct((n // seq, dq, seq), BF16)
        t_spec = pl.BlockSpec((None, dq, tm), lambda i: (i // per_seq, 0, i % per_seq))
        out_shape = [t_shape, rows_f32, rows_f32, jax.ShapeDtypeStruct((n // seq, DA_HEADS, seq, e), BF16), t_shape]
        out_specs = [t_spec, row_spec, row_spec,
                     pl.BlockSpec((None, DA_HEADS, tm, e), lambda i: (i // per_seq, 0, i % per_seq, 0)), t_spec]
    return pl.pallas_call(
        functools.partial(_qkv_kernel, scale=scale),
        grid=(n // tm,),
        in_specs=[pl.BlockSpec((tm, d), lambda i: (i, 0)), _const_spec((1, d)), _const_spec((d, 3 * dq))],
        out_specs=out_specs,
        out_shape=out_shape,
        compiler_params=_params(1),
        name="qkv",
    )(x2d, g.reshape(1, d), w_qkv.astype(BF16))


def _subln(o, ng, lam_init):
    return o * lax.rsqrt(jnp.mean(o * o, axis=-1, keepdims=True) + NORM_EPS) * ng * (1.0 - lam_init)


def _attn_prompt_kernel(lq1_ref, lk1_ref, lq2_ref, lk2_ref, ng_ref, qt_ref, k_ref, vt_ref, o_ref, sa_ref, sb_ref,
                        *, tq, tk, lam_init):
    qi = pl.program_id(2)
    e = qt_ref.shape[0]
    qt = qt_ref[...].astype(F32)
    row = lax.broadcasted_iota(jnp.int32, qt.shape, 0)
    qqt = jnp.concatenate([jnp.where(row < e // 2, qt, 0.0), jnp.where(row >= e // 2, qt, 0.0)],
                          axis=1).astype(BF16)

    def scores(c, dst_ref):
        start = pl.multiple_of(c * tk, tk)
        dst_ref[...] = _dot(k_ref[pl.ds(start, tk), :], qqt)

    def absorb(src_ref, c, carry, masked):
        m, l, acc = carry
        start = pl.multiple_of(c * tk, tk)
        s = src_ref[...]
        if masked:
            key = lax.broadcasted_iota(jnp.int32, s.shape, 0) + start
            col = lax.broadcasted_iota(jnp.int32, s.shape, 1)
            s = jnp.where(key <= jnp.where(col >= tq, col - tq, col) + qi * tq, s, -jnp.inf)
        m_new = jnp.maximum(m, jnp.max(s, axis=0, keepdims=True))
        p = jnp.exp2(s - m_new)
        alpha = jnp.exp2(m - m_new)
        l = alpha * l + jnp.sum(p, axis=0, keepdims=True)
        return m_new, l, alpha * acc + _dot(vt_ref[:, pl.ds(start, tk)], p.astype(BF16))

    init = (jnp.full((1, 2 * tq), -jnp.inf, F32), jnp.zeros((1, 2 * tq), F32), jnp.zeros((e, 2 * tq), F32))
    n = (qi * tq) // tk
    scores(0, sa_ref)

    def two_chunks(i, carry):
        scores(2 * i + 1, sb_ref)
        carry = absorb(sa_ref, 2 * i, carry, False)
        scores(2 * i + 2, sa_ref)
        return absorb(sb_ref, 2 * i + 1, carry, False)

    carry = lax.fori_loop(0, n // 2, two_chunks, init)

    def tail_even(carry):
        return absorb(sa_ref, n, carry, True)

    def tail_odd(carry):
        scores(n, sb_ref)
        carry = absorb(sa_ref, n - 1, carry, False)
        return absorb(sb_ref, n, carry, True)

    _, l, acc = lax.cond(n % 2 == 0, tail_even, tail_odd, carry)
    o = acc * (1.0 / l)
    lam = _diff_lambda(lq1_ref, lk1_ref, lq2_ref, lk2_ref, lam_init)
    o = (o[:, :tq] - lam * o[:, tq:]).T
    o_ref[...] = _subln(o, ng_ref[...], lam_init).astype(o_ref.dtype)


def _attn_prompt(qt, k, vt, lqk, norm_g, lam_init, tq, tk):
    batch, dq, seq = qt.shape
    e = dq // DA_HEADS
    dl = lqk[0].shape[0]
    assert seq % tk == 0 and tk % tq == 0
    qt_spec = pl.BlockSpec((None, e, tq), lambda b, h, i: (b, h, i))
    k_spec = pl.BlockSpec((None, None, seq, e), lambda b, h, i: (b, h, 0, 0))
    vt_spec = pl.BlockSpec((None, e, seq), lambda b, h, i: (b, h, 0))
    return pl.pallas_call(
        functools.partial(_attn_prompt_kernel, tq=tq, tk=tk, lam_init=lam_init),
        grid=(batch, DA_HEADS, seq // tq),
        in_specs=[_const_spec((1, dl))] * 4 + [pl.BlockSpec((1, e), lambda b, h, i: (0, h)), qt_spec, k_spec, vt_spec],
        out_specs=pl.BlockSpec((None, tq, e), lambda b, h, i: (b, i, h)),
        out_shape=jax.ShapeDtypeStruct((batch, seq, dq), BF16),
        scratch_shapes=[pltpu.VMEM((tk, 2 * tq), F32), pltpu.VMEM((tk, 2 * tq), F32)],
        compiler_params=_params(3),
        name="attn_prompt",
    )(*[a.reshape(1, dl) for a in lqk], norm_g.reshape(1, dq), qt, k, vt)


def _attn_sample_kernel(pt_ref, lq1_ref, lk1_ref, lq2_ref, lk2_ref, ng_ref, q_ref, kn_ref, vn_ref, *rest,
                        n_pages, lam_init):
    del pt_ref
    k_pages = rest[:n_pages]
    v_pages = rest[n_pages:2 * n_pages]
    o_ref, s_ref = rest[2 * n_pages:]
    t, nh, e = q_ref.shape
    page = k_pages[0].shape[0]
    rows = t * nh
    cols = page * nh
    nt = (((1,), (1,)), ((), ()))

    q2 = q_ref[...].reshape(rows, e)
    lane = lax.broadcasted_iota(jnp.int32, q2.shape, 1)
    wq = jnp.concatenate([jnp.where(lane < e // 2, q2, 0.0), jnp.where(lane >= e // 2, q2, 0.0)], axis=0).astype(BF16)

    r = lax.broadcasted_iota(jnp.int32, (2 * rows, cols), 0)
    c = lax.broadcasted_iota(jnp.int32, (2 * rows, cols), 1)
    same_head = _imod(c - r, nh) == 0

    rn = lax.broadcasted_iota(jnp.int32, (2 * rows, rows), 0)
    cn = lax.broadcasted_iota(jnp.int32, (2 * rows, rows), 1)
    keep_new = (_imod(cn - rn, nh) == 0) & (_idiv(cn, nh) <= _idiv(_imod(rn, rows), nh))
    s_new = lax.dot_general(wq, kn_ref[...].reshape(rows, e).astype(BF16), nt, preferred_element_type=F32)
    s_new = jnp.where(keep_new, s_new, -jnp.inf)

    mx = None
    for p in range(n_pages):
        k2 = k_pages[p][...].reshape(cols, e).astype(BF16)
        s = lax.dot_general(wq, k2, nt, preferred_element_type=F32)
        s_ref[:, p * cols:(p + 1) * cols] = s
        mx = s if mx is None else jnp.maximum(mx, s)
    mx = jnp.where(same_head, mx, -jnp.inf)
    m = jnp.maximum(jnp.max(mx, axis=-1, keepdims=True), jnp.max(s_new, axis=-1, keepdims=True))

    shift = jnp.where(same_head, -m, -jnp.inf)
    p_new = jnp.exp2(s_new - m)
    acc = _dot(p_new.astype(BF16), vn_ref[...].reshape(rows, e).astype(BF16))
    tot = None
    for p in range(n_pages):
        pe = jnp.exp2(s_ref[:, p * cols:(p + 1) * cols] + shift)
        tot = pe if tot is None else tot + pe
        acc = acc + _dot(pe.astype(BF16), v_pages[p][...].reshape(cols, e).astype(BF16))
    l = jnp.sum(tot, axis=-1, keepdims=True) + jnp.sum(p_new, axis=-1, keepdims=True)

    o = acc * (1.0 / l)
    lam = _diff_lambda(lq1_ref, lk1_ref, lq2_ref, lk2_ref, lam_init)
    o = o[:rows] - lam * o[rows:]
    ng = jnp.concatenate([ng_ref[...]] * t, axis=0)
    o_ref[...] = _subln(o, ng, lam_init).reshape(t, nh, e)


def _attn_sample(q, k_new, v_new, cache_k, cache_v, page_table, lqk, norm_g, lam_init):
    batch, t, nh, e = q.shape
    page = cache_k.shape[1]
    n_pages = page_table.shape[1]
    dl = lqk[0].shape[0]
    row_spec = pl.BlockSpec((None, t, nh, e), lambda b, pt: (b, 0, 0, 0))

    def page_spec(p):
        return pl.BlockSpec((None, page, nh, e), lambda b, pt: (pt[b, p], 0, 0, 0))

    page_specs = [page_spec(p) for p in range(n_pages)]
    grid_spec = pltpu.PrefetchScalarGridSpec(
        num_scalar_prefetch=1,
        grid=(batch,),
        in_specs=[_const_spec((1, dl))] * 4 + [_const_spec((nh, e)), row_spec, row_spec, row_spec]
        + page_specs + page_specs,
        out_specs=row_spec,
        scratch_shapes=[pltpu.VMEM((2 * t * nh, n_pages * page * nh), F32)],
    )
    return pl.pallas_call(
        functools.partial(_attn_sample_kernel, n_pages=n_pages, lam_init=lam_init),
        grid_spec=grid_spec,
        out_shape=jax.ShapeDtypeStruct((batch, t, nh, e), F32),
        compiler_params=_params(1),
        name="attn_sample",
    )(page_table, *[a.reshape(1, dl) for a in lqk], norm_g.reshape(nh, e), q, k_new, v_new,
      *([cache_k] * n_pages), *([cache_v] * n_pages))


def _pool_kernel(x_ref, st_ref, g_ref, w_ref, sc_ref, o_ref, sto_ref, hp_ref, lv_ref, *, bb, tt, windows, pos0,
                 hist, hp):
    t = pl.program_id(1)
    d = x_ref.shape[-1]
    cg = d // len(windows)
    rows = bb * tt
    lead = SUBLANES

    @pl.when(t == 0)
    def _():
        hp_ref[:, :hp - hist, :] = jnp.zeros((bb, hp - hist, d), F32)
        hp_ref[:, hp - hist:hp, :] = st_ref[...]
        lv_ref[:, :, :lead, :] = jnp.zeros((2, bb, lead, cg), F32)

    x = x_ref[...].reshape(rows, d)
    h = _rms_rows(x, g_ref[...]).reshape(bb, tt, d)
    hp_ref[:, hp:hp + tt, :] = h
    pos = pos0 + t * tt + lax.broadcasted_iota(jnp.int32, (1, tt, 1), 1)
    ys = []
    for gi, win in enumerate(windows):
        assert win & (win - 1) == 0 and win <= hp - lead
        c0 = gi * cg
        cur = h[:, :, c0:c0 + cg]

        def level_rows(level, lo, hi):
            if level == 0:
                return hp_ref[:, lo:hi, c0:c0 + cg]
            return lv_ref[level % 2, :, lo:hi, :]

        n_levels = win.bit_length() - 1
        for level in range(n_levels - 1):
            w = 1 << level
            lv_ref[(level + 1) % 2, :, lead:hp + tt, :] = (level_rows(level, lead, hp + tt)
                                                            + level_rows(level, lead - w, hp + tt - w))
        w = win // 2
        tot = level_rows(n_levels - 1, hp, hp + tt) + level_rows(n_levels - 1, hp - w, hp + tt - w)
        cnt = jnp.minimum(win, pos + 1).astype(F32)
        pooled = (tot / cnt - cur).reshape(rows, cg).astype(BF16)
        ys.append(_dot(pooled, w_ref[gi]))
    y = jnp.concatenate(ys, axis=-1) * sc_ref[...]
    o_ref[...] = (x + y).reshape(bb, tt, d)

    new_hist = hp_ref[:, tt + hp - hist:tt + hp, :]
    hp_ref[:, hp - hist:hp, :] = new_hist

    @pl.when(t == pl.num_programs(1) - 1)
    def _():
        sto_ref[...] = new_hist


def _pool_mixer(x, state, g, w_grp, scale, pos0):
    batch, seq, d = x.shape
    hist = max(POOL_WINDOWS) - 1
    hp = _round_up(hist, SUBLANES) + SUBLANES
    n_grp, cg, _ = w_grp.shape
    bb, tt = _plan_bt(batch, seq, short_rows=ROWS_PER_STEP // 2)
    x_spec = pl.BlockSpec((bb, tt, d), lambda b, t: (b, t, 0))
    st_spec = pl.BlockSpec((bb, hist, d), lambda b, t: (b, 0, 0))
    return pl.pallas_call(
        functools.partial(_pool_kernel, bb=bb, tt=tt, windows=POOL_WINDOWS, pos0=pos0, hist=hist, hp=hp),
        grid=(batch // bb, seq // tt),
        in_specs=[x_spec, st_spec, _const_spec((1, d)), _const_spec((n_grp, cg, cg)), _const_spec((1, d))],
        out_specs=[x_spec, st_spec],
        out_shape=[jax.ShapeDtypeStruct((batch, seq, d), F32), jax.ShapeDtypeStruct((batch, hist, d), F32)],
        scratch_shapes=[pltpu.VMEM((bb, hp + tt, d), F32), pltpu.VMEM((2, bb, hp + tt, cg), F32)],
        compiler_params=_params(2),
        name="pool_mixer",
    )(x, state, g.reshape(1, d), w_grp.astype(BF16), scale.reshape(1, d))


def _sg_kernel(x_ref, g_ref, win_ref, bin_ref, lng_ref, lnb_ref, ws_ref, bs_ref, wout_ref, o_ref, *rest,
               tm, sub, chunk):
    *maybe_v, z_ref = rest
    sg = wout_ref.shape[0]
    cg = sg // SG_GROUPS
    r = lax.broadcasted_iota(jnp.int32, (SG_CHUNK, SG_CHUNK), 0)
    c = lax.broadcasted_iota(jnp.int32, (SG_CHUNK, SG_CHUNK), 1)
    keep = (_idiv(r, chunk) == _idiv(c, chunk)) & (c <= r)
    wss = [jnp.where(keep, ws_ref[gi], 0.0).astype(BF16) for gi in range(SG_GROUPS)]

    def project(i):
        h = _rms_rows(x_ref[i * sub:(i + 1) * sub, :], g_ref[...]).astype(BF16)
        z_ref[i % 2] = _dot(h, win_ref[...])

    def finish(i):
        rows = slice(i * sub, (i + 1) * sub)
        z = _gelu_tanh(z_ref[i % 2] + bin_ref[...])
        u = z[:, :sg]
        v = _layernorm_rows(z[:, sg:], lng_ref[...], lnb_ref[...])
        if maybe_v:
            maybe_v[0][rows, :] = v
        vb = v.astype(BF16)
        acc = x_ref[rows, :]
        for gi in range(SG_GROUPS):
            parts = []
            for r0 in range(0, sub, SG_CHUNK):
                parts.append(_dot(wss[gi], vb[r0:r0 + SG_CHUNK, gi * cg:(gi + 1) * cg]) + bs_ref[gi])
            s = jnp.concatenate(parts, axis=0) if len(parts) > 1 else parts[0]
            gated = (u[:, gi * cg:(gi + 1) * cg] * s).astype(BF16)
            acc = acc + _dot(gated, wout_ref[gi * cg:(gi + 1) * cg, :])
        o_ref[rows, :] = acc

    n_sub = tm // sub
    project(0)
    for i in range(n_sub):
        if i + 1 < n_sub:
            project(i + 1)
        finish(i)


def _sg_mixer(x2d, g, w_in, b_in, ln_g, ln_b, w_s, b_s, w_out, chunk, emit_v, tm):
    n, d = x2d.shape
    sg = w_out.shape[0]
    reps = SG_CHUNK // chunk
    ws = jnp.tile(w_s[:, :chunk, :chunk], (1, reps, reps))
    bs = jnp.tile(b_s[:, :chunk], (1, reps)).reshape(SG_GROUPS, SG_CHUNK, 1)
    row_spec = pl.BlockSpec((tm, d), lambda i: (i, 0))
    out_shape = [jax.ShapeDtypeStruct((n, d), F32)]
    out_specs = [row_spec]
    if emit_v:
        out_shape.append(jax.ShapeDtypeStruct((n, sg), F32))
        out_specs.append(pl.BlockSpec((tm, sg), lambda i: (i, 0)))
    return pl.pallas_call(
        functools.partial(_sg_kernel, tm=tm, sub=min(tm, SG_SUB_ROWS), chunk=chunk),
        grid=(n // tm,),
        in_specs=[row_spec, _const_spec((1, d)), _const_spec((d, 2 * sg)), _const_spec((1, 2 * sg)),
                  _const_spec((1, sg)), _const_spec((1, sg)), _const_spec((SG_GROUPS, SG_CHUNK, SG_CHUNK)),
                  _const_spec((SG_GROUPS, SG_CHUNK, 1)), _const_spec((sg, d))],
        out_specs=out_specs,
        out_shape=out_shape,
        scratch_shapes=[pltpu.VMEM((2, min(tm, SG_SUB_ROWS), 2 * sg), F32)],
        compiler_params=_params(1),
        name="sg_mixer",
    )(x2d, g.reshape(1, d), w_in.astype(BF16), b_in.reshape(1, 2 * sg), ln_g.reshape(1, sg), ln_b.reshape(1, sg),
      ws, bs, w_out.astype(BF16))


def kernel(x_prompt, x_sample, state_conv, cache_k, cache_v, page_table, state_pool, state_ffn, norm_mix, norm_ffn, norm_final, cv_w_in, cv_b_in, cv_w_dw, cv_b_dw, cv_ln_g, cv_ln_b, cv_w_out, cv_b_out, da_w_qkv, da_lq1, da_lk1, da_lq2, da_lk2, da_norm_g, da_w_o, pl_w, pl_scale, sg_w_in, sg_b_in, sg_ln_g, sg_ln_b, sg_w_s, sg_b_s, sg_w_out, ff_w_gate, ff_w_up, ff_w_dw, ff_b_dw, ff_w_down):
    bp, seq, d = x_prompt.shape
    bs, dec_seq, _ = x_sample.shape
    depth, ffn_kw, d_ff = ff_w_dw.shape
    past_len = page_table.shape[1] * cache_k.shape[1]
    head_dim = d // (2 * DA_HEADS)
    kv_shape = (DA_HEADS, 2 * head_dim)

    wg_all, wu_all, wd_all = ff_w_gate.astype(BF16), ff_w_up.astype(BF16), ff_w_down.astype(BF16)

    def ffn(i, x, state, **kw):
        return _ffn(x, state, norm_ffn[i], wg_all, wu_all, ff_w_dw[i], ff_b_dw[i], wd_all, i, **kw)

    ffn_zero = jnp.zeros((bp, ffn_kw - 1, d_ff), F32)
    ffn_p, ffn_s = [], []

    cw = (norm_mix[0], cv_w_in, cv_b_in, cv_w_dw, cv_b_dw, cv_ln_g, cv_ln_b, cv_w_out, cv_b_out)
    xp, conv_p = _conv_mixer(x_prompt, jnp.zeros((bp,) + state_conv.shape[1:], F32), *cw)
    xs, conv_s = _conv_mixer(x_sample, state_conv, *cw)
    xp, st = ffn(0, xp, ffn_zero)
    ffn_p.append(st)
    xs, st = ffn(0, xs, state_ffn[0])
    ffn_s.append(st)

    lam_init = 0.8 - 0.6 * math.exp(-0.3 * 1)
    lqk = (da_lq1, da_lk1, da_lq2, da_lk2)
    scale = head_dim ** -0.5 * math.log2(math.e)
    qt_p, k_rows_p, v_rows_p, kh_p, vt_p = _qkv(xp.reshape(bp * seq, d), norm_mix[1], da_w_qkv, scale, seq=seq)
    o_p = _attn_prompt(qt_p, kh_p, vt_p, lqk, da_norm_g, lam_init, tq=min(seq, ATTN_TQ), tk=min(seq, ATTN_TK))
    xp, st = ffn(1, xp, ffn_zero, proj=(o_p, da_w_o))
    ffn_p.append(st)
    q_s, k_rows_s, v_rows_s = _qkv(xs.reshape(bs * dec_seq, d), norm_mix[1], da_w_qkv, scale)
    rows_s = (bs, dec_seq) + kv_shape
    o_s = _attn_sample(q_s.reshape(rows_s), k_rows_s.reshape(rows_s), v_rows_s.reshape(rows_s),
                       cache_k, cache_v, page_table, lqk, da_norm_g, lam_init)
    xs, st = ffn(1, xs, state_ffn[1], proj=(o_s.reshape(bs, dec_seq, d), da_w_o))
    ffn_s.append(st)

    xp, pool_p = _pool_mixer(xp, jnp.zeros((bp,) + state_pool.shape[1:], F32), norm_mix[2], pl_w, pl_scale, 0)
    xs, pool_s = _pool_mixer(xs, state_pool, norm_mix[2], pl_w, pl_scale, past_len)
    xp, st = ffn(2, xp, ffn_zero)
    ffn_p.append(st)
    xs, st = ffn(2, xs, state_ffn[2])
    ffn_s.append(st)

    sw = (norm_mix[3], sg_w_in, sg_b_in, sg_ln_g, sg_ln_b, sg_w_s, sg_b_s, sg_w_out)
    assert seq % SG_CHUNK == 0 and past_len % SG_CHUNK == 0 and SG_CHUNK % dec_seq == 0
    (xp2,) = _sg_mixer(xp.reshape(bp * seq, d), *sw, chunk=SG_CHUNK, emit_v=False, tm=min(seq, 2 * SG_SUB_ROWS))
    xs2, sg_v = _sg_mixer(xs.reshape(bs * dec_seq, d), *sw, chunk=dec_seq, emit_v=True, tm=2 * LANES)
    y_prompt, st = ffn(3, xp2.reshape(bp, seq, d), ffn_zero, final_g=norm_final)
    ffn_p.append(st)
    y_sample, st = ffn(3, xs2.reshape(bs, dec_seq, d), state_ffn[3], final_g=norm_final)
    ffn_s.append(st)

    return (y_prompt, y_sample, conv_p, conv_s,
            k_rows_p.reshape((bp, seq) + kv_shape), v_rows_p.reshape((bp, seq) + kv_shape),
            k_rows_s.reshape((bs, dec_seq) + kv_shape), v_rows_s.reshape((bs, dec_seq) + kv_shape),
            pool_p, pool_s, sg_v.reshape(bs, dec_seq, -1), jnp.stack(ffn_p, axis=0), jnp.stack(ffn_s, axis=0))
```

```python
import functools
import math

import jax
import jax.numpy as jnp
from jax import lax
from jax.experimental import pallas as pl
from jax.experimental.pallas import tpu as pltpu

F32 = jnp.float32
BF16 = jnp.bfloat16
NORM_EPS = 1e-6
POOL_WINDOWS = (2, 4, 8, 16)
SG_CHUNK = 128
SG_GROUPS = 4
DA_HEADS = 8

SUBLANES = 8
LANES = 128
MXU_DIM = 256
VMEM_LIMIT_BYTES = 56 * 1024 * 1024
ROWS_PER_STEP = 512
SG_SUB_ROWS = 256
ATTN_TILE = 512


def _round_up(n, m):
    return -(-n // m) * m


def _plan_bt(batch, seq, short_rows):
    if seq >= ROWS_PER_STEP:
        assert seq % ROWS_PER_STEP == 0
        return 1, ROWS_PER_STEP
    bb = max(1, min(batch, short_rows // seq))
    assert batch % bb == 0 and seq % SUBLANES == 0
    return bb, seq


def _const_spec(shape):
    nd = len(shape)
    return pl.BlockSpec(shape, lambda *_: (0,) * nd, pipeline_mode=pl.Buffered(1))


def _params(n_grid):
    return pltpu.CompilerParams(dimension_semantics=("arbitrary",) * n_grid,
                                vmem_limit_bytes=VMEM_LIMIT_BYTES)


def _rms_rows(x, g):
    return x * lax.rsqrt(jnp.mean(x * x, axis=-1, keepdims=True) + NORM_EPS) * g


def _layernorm_rows(x, g, b):
    mu = jnp.mean(x, axis=-1, keepdims=True)
    xc = x - mu
    return xc * lax.rsqrt(jnp.mean(xc * xc, axis=-1, keepdims=True) + NORM_EPS) * g + b


def _sigmoid(x):
    return 1.0 / (1.0 + jnp.exp(-x))


def _silu(x):
    return x * _sigmoid(x)


def _gelu_tanh(x):
    return x * (0.5 * (1.0 + jnp.tanh(math.sqrt(2.0 / math.pi) * (x + 0.044715 * (x * x * x)))))


def _idiv(x, n):
    assert n & (n - 1) == 0
    return x >> (n.bit_length() - 1)


def _imod(x, n):
    assert n & (n - 1) == 0
    return x & (n - 1)


def _dot(a, b):
    return jnp.dot(a, b, preferred_element_type=F32)


def _diff_lambda(lq1_ref, lk1_ref, lq2_ref, lk2_ref, lam_init):
    a = jnp.sum(lq1_ref[...] * lk1_ref[...], axis=-1, keepdims=True)
    b = jnp.sum(lq2_ref[...] * lk2_ref[...], axis=-1, keepdims=True)
    return jnp.exp(a) - jnp.exp(b) + lam_init


def _conv_mixer_kernel(x_ref, st_ref, g_ref, win_ref, bin_ref, wdw_ref, bdw_ref, lng_ref, lnb_ref,
                       wout_ref, bout_ref, o_ref, sto_ref, gp_ref, c_ref, sh_ref, *, bb, tt, kw, hp, sb, rb, cb):
    hist = kw - 1
    t = pl.program_id(1)
    d = x_ref.shape[-1]
    c = gp_ref.shape[-1]
    rows = bb * tt

    @pl.when(t == 0)
    def _():
        gp_ref[:, hp - hist:hp, :] = st_ref[...]

    x = x_ref[...].reshape(rows, d)
    h = _rms_rows(x, g_ref[...]).astype(BF16)
    ag = _dot(h, win_ref[...]) + bin_ref[...]
    glu = ag[:, :c] * _sigmoid(ag[:, c:])
    gp_ref[:, hp:hp + tt, :] = glu.reshape(bb, tt, c)

    base = hp - hist
    span = sh_ref.shape[2]
    for c0 in range(0, c, cb):
        for r in range(1, SUBLANES):
            sh_ref[r - 1] = gp_ref[:, r:r + span, c0:c0 + cb]
        for b0 in range(0, bb, sb):
            for r0 in range(0, tt, rb):
                acc = jnp.broadcast_to(bdw_ref[:, c0:c0 + cb].reshape(1, 1, cb), (sb, rb, cb))
                for k in range(kw):
                    a, r = divmod(base + k, SUBLANES)
                    lo = r0 + SUBLANES * a
                    if r == 0:
                        win = gp_ref[b0:b0 + sb, lo:lo + rb, c0:c0 + cb]
                    else:
                        win = sh_ref[r - 1, b0:b0 + sb, lo:lo + rb, :]
                    acc = acc + win * wdw_ref[k:k + 1, c0:c0 + cb].reshape(1, 1, cb)
                c_ref[b0:b0 + sb, r0:r0 + rb, c0:c0 + cb] = acc

    conv = c_ref[...].reshape(rows, c)
    act = _silu(_layernorm_rows(conv, lng_ref[...], lnb_ref[...])).astype(BF16)
    m = _dot(act, wout_ref[...]) + bout_ref[...]
    o_ref[...] = (x + m).reshape(bb, tt, d)

    new_hist = gp_ref[:, tt + hp - hist:tt + hp, :]
    gp_ref[:, hp - hist:hp, :] = new_hist

    @pl.when(t == pl.num_programs(1) - 1)
    def _():
        sto_ref[...] = new_hist


def _conv_mixer(x, state, g, w_in, b_in, w_dw, b_dw, ln_g, ln_b, w_out, b_out):
    batch, seq, d = x.shape
    kw, c = w_dw.shape
    hist = kw - 1
    hp = _round_up(hist, SUBLANES)
    bb, tt = _plan_bt(batch, seq, short_rows=ROWS_PER_STEP // 4)
    cb = 2 * LANES
    rb = min(tt, LANES)
    sb = max(1, min(bb, LANES // rb))
    kern = functools.partial(_conv_mixer_kernel, bb=bb, tt=tt, kw=kw, hp=hp, sb=sb, rb=rb, cb=cb)
    x_spec = pl.BlockSpec((bb, tt, d), lambda b, t: (b, t, 0))
    st_spec = pl.BlockSpec((bb, hist, c), lambda b, t: (b, 0, 0))
    return pl.pallas_call(
        kern,
        grid=(batch // bb, seq // tt),
        in_specs=[x_spec, st_spec, _const_spec((1, d)), _const_spec((d, 2 * c)), _const_spec((1, 2 * c)),
                  _const_spec((kw, c)), _const_spec((1, c)), _const_spec((1, c)), _const_spec((1, c)),
                  _const_spec((c, d)), _const_spec((1, d))],
        out_specs=[x_spec, st_spec],
        out_shape=[jax.ShapeDtypeStruct((batch, seq, d), F32), jax.ShapeDtypeStruct((batch, hist, c), F32)],
        scratch_shapes=[pltpu.VMEM((bb, hp + tt, c), F32), pltpu.VMEM((bb, tt, c), F32),
                        pltpu.VMEM((SUBLANES - 1, bb, hp + tt - SUBLANES, cb), F32)],
        compiler_params=_params(2),
        name="conv_mixer",
    )(x, state, g.reshape(1, d), w_in.astype(BF16), b_in.reshape(1, 2 * c), w_dw, b_dw.reshape(1, c),
      ln_g.reshape(1, c), ln_b.reshape(1, c), w_out.astype(BF16), b_out.reshape(1, d))


def _ffn_kernel(*refs, bb, tt, proj, final, chunks, hp):
    refs = list(refs)
    x_ref = refs.pop(0)
    if proj:
        m_ref = refs.pop(0)
        wp_ref = refs.pop(0)
    st_ref, g_ref, wg_ref, wu_ref, wdw_ref, bdw_ref, wd_ref = refs[:7]
    refs = refs[7:]
    if final:
        gf_ref = refs.pop(0)
    o_ref, sto_ref, gpad_ref, carry_ref = refs
    hist = wdw_ref.shape[0] - 1
    t = pl.program_id(1)
    d = x_ref.shape[-1]
    rows = bb * tt

    @pl.when(t == 0)
    def _():
        carry_ref[...] = st_ref[...]

    x = x_ref[...].reshape(rows, d)
    if proj:
        x = x + _dot(m_ref[...].reshape(rows, m_ref.shape[-1]).astype(BF16), wp_ref[...])
    h = _rms_rows(x, g_ref[...]).astype(BF16)
    acc = x
    base = hp - hist
    for c0, fc in chunks:
        gate = _dot(h, wg_ref[:, c0:c0 + fc]).reshape(bb, tt, fc)
        up = _dot(h, wu_ref[:, c0:c0 + fc])
        gpad_ref[:, base:hp, :fc] = carry_ref[:, :, c0:c0 + fc]
        gpad_ref[:, hp:hp + tt, :fc] = gate
        gc = gate * wdw_ref[hist:hist + 1, c0:c0 + fc].reshape(1, 1, fc) + bdw_ref[:, c0:c0 + fc].reshape(1, 1, fc)
        for k in range(hist):
            gc = gc + gpad_ref[:, base + k:base + k + tt, :fc] * wdw_ref[k:k + 1, c0:c0 + fc].reshape(1, 1, fc)
        carry_ref[:, :, c0:c0 + fc] = gpad_ref[:, tt + base:tt + hp, :fc]
        act = (_silu(gc).reshape(rows, fc) * up).astype(BF16)
        acc = acc + _dot(act, wd_ref[c0:c0 + fc, :])
    if final:
        acc = _rms_rows(acc, gf_ref[...])
    o_ref[...] = acc.reshape(bb, tt, d)
    sto_ref[...] = carry_ref[...]


def _ffn(x, state, g, w_gate, w_up, w_dw, b_dw, w_down, layer, proj=None, final_g=None):
    batch, seq, d = x.shape
    kw, f = w_dw.shape

    def layer_spec(shape):
        return pl.BlockSpec((None,) + shape, lambda *_: (layer, 0, 0), pipeline_mode=pl.Buffered(1))

    hist = kw - 1
    hp = _round_up(hist, SUBLANES)
    bb, tt = _plan_bt(batch, seq, short_rows=ROWS_PER_STEP // 2)
    n_tiles = f // MXU_DIM
    assert f % MXU_DIM == 0
    half = (n_tiles + 1) // 2 * MXU_DIM
    chunks = ((0, half), (half, f - half)) if f > half else ((0, f),)
    kern = functools.partial(_ffn_kernel, bb=bb, tt=tt, proj=proj is not None, final=final_g is not None,
                             chunks=chunks, hp=hp)
    x_spec = pl.BlockSpec((bb, tt, d), lambda b, t: (b, t, 0))
    st_spec = pl.BlockSpec((bb, hist, f), lambda b, t: (b, 0, 0))
    args, specs = [x], [x_spec]
    if proj is not None:
        m, w_proj = proj
        dm = m.shape[-1]
        args += [m, w_proj.astype(BF16)]
        specs += [pl.BlockSpec((bb, tt, dm), lambda b, t: (b, t, 0)), _const_spec((dm, d))]
    args += [state, g.reshape(1, d), w_gate, w_up, w_dw, b_dw.reshape(1, f), w_down]
    specs += [st_spec, _const_spec((1, d)), layer_spec((d, f)), layer_spec((d, f)), _const_spec((kw, f)),
              _const_spec((1, f)), layer_spec((f, d))]
    if final_g is not None:
        args.append(final_g.reshape(1, d))
        specs.append(_const_spec((1, d)))
    return pl.pallas_call(
        kern,
        grid=(batch // bb, seq // tt),
        in_specs=specs,
        out_specs=[x_spec, st_spec],
        out_shape=[jax.ShapeDtypeStruct((batch, seq, d), F32), jax.ShapeDtypeStruct((batch, hist, f), F32)],
        scratch_shapes=[pltpu.VMEM((bb, hp + tt, chunks[0][1]), F32), pltpu.VMEM((bb, hist, f), F32)],
        compiler_params=_params(2),
        name="conv_ffn",
    )(*args)


def _qkv_kernel(x_ref, g_ref, w_ref, q_ref, k_ref, v_ref, *maybe_heads, scale):
    dq = k_ref.shape[-1]
    h = _rms_rows(x_ref[...], g_ref[...]).astype(BF16)
    qkv = _dot(h, w_ref[...])
    q = qkv[:, :dq] * scale
    k = qkv[:, dq:2 * dq]
    v = qkv[:, 2 * dq:]
    k_ref[...] = k
    v_ref[...] = v
    if maybe_heads:
        kh_ref, vt_ref = maybe_heads
        q_ref[...] = q.T.astype(BF16)
        vt_ref[...] = v.T.astype(BF16)
        e = kh_ref.shape[-1]
        for hd in range(kh_ref.shape[0]):
            kh_ref[hd] = k[:, hd * e:(hd + 1) * e].astype(BF16)
    else:
        q_ref[...] = q


def _qkv(x2d, g, w_qkv, scale, seq=None):
    n, d = x2d.shape
    dq = w_qkv.shape[1] // 3
    tm = min(n, ROWS_PER_STEP)
    row_spec = pl.BlockSpec((tm, dq), lambda i: (i, 0))
    rows_f32 = jax.ShapeDtypeStruct((n, dq), F32)
    if seq is None:
        out_shape = [rows_f32] * 3
        out_specs = [row_spec] * 3
    else:
        assert seq % tm == 0
        per_seq = seq // tm
        e = dq // DA_HEADS
        t_shape = jax.ShapeDtypeStruct((n // seq, dq, seq), BF16)
        t_spec = pl.BlockSpec((None, dq, tm), lambda i: (i // per_seq, 0, i % per_seq))
        out_shape = [t_shape, rows_f32, rows_f32, jax.ShapeDtypeStruct((n // seq, DA_HEADS, seq, e), BF16), t_shape]
        out_specs = [t_spec, row_spec, row_spec,
                     pl.BlockSpec((None, DA_HEADS, tm, e), lambda i: (i // per_seq, 0, i % per_seq, 0)), t_spec]
    return pl.pallas_call(
        functools.partial(_qkv_kernel, scale=scale),
        grid=(n // tm,),
        in_specs=[pl.BlockSpec((tm, d), lambda i: (i, 0)), _const_spec((1, d)), _const_spec((d, 3 * dq))],
        out_specs=out_specs,
        out_shape=out_shape,
        compiler_params=_params(1),
        name="qkv",
    )(x2d, g.reshape(1, d), w_qkv.astype(BF16))


def _subln(o, ng, lam_init):
    return o * lax.rsqrt(jnp.mean(o * o, axis=-1, keepdims=True) + NORM_EPS) * ng * (1.0 - lam_init)


def _attn_prompt_kernel(lq1_ref, lk1_ref, lq2_ref, lk2_ref, ng_ref, qt_ref, k_ref, vt_ref, o_ref, sa_ref, sb_ref,
                        *, tq, lam_init):
    j = pl.program_id(2)
    e = qt_ref.shape[0]
    w = 2 * tq
    qt = qt_ref[...].astype(F32)
    row = lax.broadcasted_iota(jnp.int32, qt.shape, 0)
    lo = jnp.where(row < e // 2, qt, 0.0)
    hi = jnp.where(row >= e // 2, qt, 0.0)
    qq = jnp.concatenate([lo[:, :tq], hi[:, :tq], lo[:, tq:], hi[:, tq:]], axis=1).astype(BF16)

    def scores(c, dst_ref, first_col):
        start = pl.multiple_of(c * tq, tq)
        dst_ref[:, first_col:] = _dot(k_ref[pl.ds(start, tq), :], qq[:, first_col:])

    def absorb(src_ref, c, carry, tile, own_chunk):
        m, l, acc = carry
        start = pl.multiple_of(c * tq, tq)
        s = src_ref[:, tile * w:(tile + 1) * w]
        if own_chunk:
            key = lax.broadcasted_iota(jnp.int32, s.shape, 0)
            col = lax.broadcasted_iota(jnp.int32, s.shape, 1)
            s = jnp.where(key <= jnp.where(col >= tq, col - tq, col), s, -jnp.inf)
        m_new = jnp.maximum(m, jnp.max(s, axis=0, keepdims=True))
        p = jnp.exp2(s - m_new)
        alpha = jnp.exp2(m - m_new)
        l = alpha * l + jnp.sum(p, axis=0, keepdims=True)
        return m_new, l, alpha * acc + _dot(vt_ref[:, pl.ds(start, tq)], p.astype(BF16))

    init = (jnp.full((1, w), -jnp.inf, F32), jnp.zeros((1, w), F32), jnp.zeros((e, w), F32))
    scores(0, sa_ref, 0)

    def two_chunks(i, carries):
        ce, co = carries
        scores(2 * i + 1, sb_ref, 0)
        ce = absorb(sa_ref, 2 * i, ce, 0, False)
        co = absorb(sa_ref, 2 * i, co, 1, False)
        scores(2 * i + 2, sa_ref, 0)
        ce = absorb(sb_ref, 2 * i + 1, ce, 0, False)
        co = absorb(sb_ref, 2 * i + 1, co, 1, False)
        return ce, co

    ce, co = lax.fori_loop(0, j, two_chunks, (init, init))
    scores(2 * j + 1, sb_ref, w)
    ce = absorb(sa_ref, 2 * j, ce, 0, True)
    co = absorb(sa_ref, 2 * j, co, 1, False)
    co = absorb(sb_ref, 2 * j + 1, co, 1, True)

    lam = _diff_lambda(lq1_ref, lk1_ref, lq2_ref, lk2_ref, lam_init)
    for tile, (_, l, acc) in enumerate((ce, co)):
        o = acc * (1.0 / l)
        o = (o[:, :tq] - lam * o[:, tq:]).T
        o_ref[tile * tq:(tile + 1) * tq, :] = _subln(o, ng_ref[...], lam_init).astype(o_ref.dtype)


def _attn_prompt(qt, k, vt, lqk, norm_g, lam_init, tq):
    batch, dq, seq = qt.shape
    e = dq // DA_HEADS
    dl = lqk[0].shape[0]
    assert seq % (2 * tq) == 0
    qt_spec = pl.BlockSpec((None, e, 2 * tq), lambda b, h, i: (b, h, i))
    k_spec = pl.BlockSpec((None, None, seq, e), lambda b, h, i: (b, h, 0, 0))
    vt_spec = pl.BlockSpec((None, e, seq), lambda b, h, i: (b, h, 0))
    return pl.pallas_call(
        functools.partial(_attn_prompt_kernel, tq=tq, lam_init=lam_init),
        grid=(batch, DA_HEADS, seq // (2 * tq)),
        in_specs=[_const_spec((1, dl))] * 4 + [pl.BlockSpec((1, e), lambda b, h, i: (0, h)), qt_spec, k_spec, vt_spec],
        out_specs=pl.BlockSpec((None, 2 * tq, e), lambda b, h, i: (b, i, h)),
        out_shape=jax.ShapeDtypeStruct((batch, seq, dq), BF16),
        scratch_shapes=[pltpu.VMEM((tq, 4 * tq), F32), pltpu.VMEM((tq, 4 * tq), F32)],
        compiler_params=_params(3),
        name="attn_prompt",
    )(*[a.reshape(1, dl) for a in lqk], norm_g.reshape(1, dq), qt, k, vt)


def _attn_sample_kernel(pt_ref, lq1_ref, lk1_ref, lq2_ref, lk2_ref, ng_ref, q_ref, kn_ref, vn_ref, *rest,
                        n_pages, lam_init):
    del pt_ref
    k_pages = rest[:n_pages]
    v_pages = rest[n_pages:2 * n_pages]
    o_ref, s_ref = rest[2 * n_pages:]
    t, nh, e = q_ref.shape
    page = k_pages[0].shape[0]
    rows = t * nh
    cols = page * nh
    nt = (((1,), (1,)), ((), ()))

    q2 = q_ref[...].reshape(rows, e)
    lane = lax.broadcasted_iota(jnp.int32, q2.shape, 1)
    wq = jnp.concatenate([jnp.where(lane < e // 2, q2, 0.0), jnp.where(lane >= e // 2, q2, 0.0)], axis=0).astype(BF16)

    r = lax.broadcasted_iota(jnp.int32, (2 * rows, cols), 0)
    c = lax.broadcasted_iota(jnp.int32, (2 * rows, cols), 1)
    same_head = _imod(c - r, nh) == 0

    rn = lax.broadcasted_iota(jnp.int32, (2 * rows, rows), 0)
    cn = lax.broadcasted_iota(jnp.int32, (2 * rows, rows), 1)
    keep_new = (_imod(cn - rn, nh) == 0) & (_idiv(cn, nh) <= _idiv(_imod(rn, rows), nh))
    s_new = lax.dot_general(wq, kn_ref[...].reshape(rows, e).astype(BF16), nt, preferred_element_type=F32)
    s_new = jnp.where(keep_new, s_new, -jnp.inf)

    mx = None
    for p in range(n_pages):
        k2 = k_pages[p][...].reshape(cols, e).astype(BF16)
        s = lax.dot_general(wq, k2, nt, preferred_element_type=F32)
        s_ref[:, p * cols:(p + 1) * cols] = s
        mx = s if mx is None else jnp.maximum(mx, s)
    mx = jnp.where(same_head, mx, -jnp.inf)
    m = jnp.maximum(jnp.max(mx, axis=-1, keepdims=True), jnp.max(s_new, axis=-1, keepdims=True))

    shift = jnp.where(same_head, -m, -jnp.inf)
    p_new = jnp.exp2(s_new - m)
    acc = _dot(p_new.astype(BF16), vn_ref[...].reshape(rows, e).astype(BF16))
    tot = None
    for p in range(n_pages):
        pe = jnp.exp2(s_ref[:, p * cols:(p + 1) * cols] + shift)
        tot = pe if tot is None else tot + pe
        acc = acc + _dot(pe.astype(BF16), v_pages[p][...].reshape(cols, e).astype(BF16))
    l = jnp.sum(tot, axis=-1, keepdims=True) + jnp.sum(p_new, axis=-1, keepdims=True)

    o = acc * (1.0 / l)
    lam = _diff_lambda(lq1_ref, lk1_ref, lq2_ref, lk2_ref, lam_init)
    o = o[:rows] - lam * o[rows:]
    ng = jnp.concatenate([ng_ref[...]] * t, axis=0)
    o_ref[...] = _subln(o, ng, lam_init).reshape(t, nh, e)


def _attn_sample(q, k_new, v_new, cache_k, cache_v, page_table, lqk, norm_g, lam_init):
    batch, t, nh, e = q.shape
    page = cache_k.shape[1]
    n_pages = page_table.shape[1]
    dl = lqk[0].shape[0]
    row_spec = pl.BlockSpec((None, t, nh, e), lambda b, pt: (b, 0, 0, 0))

    def page_spec(p):
        return pl.BlockSpec((None, page, nh, e), lambda b, pt: (pt[b, p], 0, 0, 0))

    page_specs = [page_spec(p) for p in range(n_pages)]
    grid_spec = pltpu.PrefetchScalarGridSpec(
        num_scalar_prefetch=1,
        grid=(batch,),
        in_specs=[_const_spec((1, dl))] * 4 + [_const_spec((nh, e)), row_spec, row_spec, row_spec]
        + page_specs + page_specs,
        out_specs=row_spec,
        scratch_shapes=[pltpu.VMEM((2 * t * nh, n_pages * page * nh), F32)],
    )
    return pl.pallas_call(
        functools.partial(_attn_sample_kernel, n_pages=n_pages, lam_init=lam_init),
        grid_spec=grid_spec,
        out_shape=jax.ShapeDtypeStruct((batch, t, nh, e), F32),
        compiler_params=_params(1),
        name="attn_sample",
    )(page_table, *[a.reshape(1, dl) for a in lqk], norm_g.reshape(nh, e), q, k_new, v_new,
      *([cache_k] * n_pages), *([cache_v] * n_pages))


def _pool_kernel(x_ref, st_ref, g_ref, w_ref, sc_ref, o_ref, sto_ref, hp_ref, lv_ref, *, bb, tt, windows, pos0,
                 hist, hp):
    t = pl.program_id(1)
    d = x_ref.shape[-1]
    cg = d // len(windows)
    rows = bb * tt
    lead = SUBLANES

    @pl.when(t == 0)
    def _():
        hp_ref[:, :hp - hist, :] = jnp.zeros((bb, hp - hist, d), F32)
        hp_ref[:, hp - hist:hp, :] = st_ref[...]
        lv_ref[:, :, :lead, :] = jnp.zeros((2, bb, lead, cg), F32)

    x = x_ref[...].reshape(rows, d)
    h = _rms_rows(x, g_ref[...]).reshape(bb, tt, d)
    hp_ref[:, hp:hp + tt, :] = h
    pos = pos0 + t * tt + lax.broadcasted_iota(jnp.int32, (1, tt, 1), 1)
    ys = []
    for gi, win in enumerate(windows):
        assert win & (win - 1) == 0 and win <= hp - lead
        c0 = gi * cg
        cur = h[:, :, c0:c0 + cg]

        def level_rows(level, lo, hi):
            if level == 0:
                return hp_ref[:, lo:hi, c0:c0 + cg]
            return lv_ref[level % 2, :, lo:hi, :]

        n_levels = win.bit_length() - 1
        for level in range(n_levels - 1):
            w = 1 << level
            lv_ref[(level + 1) % 2, :, lead:hp + tt, :] = (level_rows(level, lead, hp + tt)
                                                            + level_rows(level, lead - w, hp + tt - w))
        w = win // 2
        tot = level_rows(n_levels - 1, hp, hp + tt) + level_rows(n_levels - 1, hp - w, hp + tt - w)
        cnt = jnp.minimum(win, pos + 1).astype(F32)
        pooled = (tot / cnt - cur).reshape(rows, cg).astype(BF16)
        ys.append(_dot(pooled, w_ref[gi]))
    y = jnp.concatenate(ys, axis=-1) * sc_ref[...]
    o_ref[...] = (x + y).reshape(bb, tt, d)

    new_hist = hp_ref[:, tt + hp - hist:tt + hp, :]
    hp_ref[:, hp - hist:hp, :] = new_hist

    @pl.when(t == pl.num_programs(1) - 1)
    def _():
        sto_ref[...] = new_hist


def _pool_mixer(x, state, g, w_grp, scale, pos0):
    batch, seq, d = x.shape
    hist = max(POOL_WINDOWS) - 1
    hp = _round_up(hist, SUBLANES) + SUBLANES
    n_grp, cg, _ = w_grp.shape
    bb, tt = _plan_bt(batch, seq, short_rows=ROWS_PER_STEP // 2)
    x_spec = pl.BlockSpec((bb, tt, d), lambda b, t: (b, t, 0))
    st_spec = pl.BlockSpec((bb, hist, d), lambda b, t: (b, 0, 0))
    return pl.pallas_call(
        functools.partial(_pool_kernel, bb=bb, tt=tt, windows=POOL_WINDOWS, pos0=pos0, hist=hist, hp=hp),
        grid=(batch // bb, seq // tt),
        in_specs=[x_spec, st_spec, _const_spec((1, d)), _const_spec((n_grp, cg, cg)), _const_spec((1, d))],
        out_specs=[x_spec, st_spec],
        out_shape=[jax.ShapeDtypeStruct((batch, seq, d), F32), jax.ShapeDtypeStruct((batch, hist, d), F32)],
        scratch_shapes=[pltpu.VMEM((bb, hp + tt, d), F32), pltpu.VMEM((2, bb, hp + tt, cg), F32)],
        compiler_params=_params(2),
        name="pool_mixer",
    )(x, state, g.reshape(1, d), w_grp.astype(BF16), scale.reshape(1, d))


def _sg_kernel(x_ref, g_ref, win_ref, bin_ref, lng_ref, lnb_ref, ws_ref, bs_ref, wout_ref, o_ref, *rest,
               tm, sub, chunk):
    *maybe_v, z_ref = rest
    sg = wout_ref.shape[0]
    cg = sg // SG_GROUPS
    r = lax.broadcasted_iota(jnp.int32, (SG_CHUNK, SG_CHUNK), 0)
    c = lax.broadcasted_iota(jnp.int32, (SG_CHUNK, SG_CHUNK), 1)
    keep = (_idiv(r, chunk) == _idiv(c, chunk)) & (c <= r)
    wss = [jnp.where(keep, ws_ref[gi], 0.0).astype(BF16) for gi in range(SG_GROUPS)]

    def project(i):
        h = _rms_rows(x_ref[i * sub:(i + 1) * sub, :], g_ref[...]).astype(BF16)
        z_ref[i % 2] = _dot(h, win_ref[...])

    def finish(i):
        rows = slice(i * sub, (i + 1) * sub)
        z = _gelu_tanh(z_ref[i % 2] + bin_ref[...])
        u = z[:, :sg]
        v = _layernorm_rows(z[:, sg:], lng_ref[...], lnb_ref[...])
        if maybe_v:
            maybe_v[0][rows, :] = v
        vb = v.astype(BF16)
        acc = x_ref[rows, :]
        for gi in range(SG_GROUPS):
            parts = []
            for r0 in range(0, sub, SG_CHUNK):
                parts.append(_dot(wss[gi], vb[r0:r0 + SG_CHUNK, gi * cg:(gi + 1) * cg]) + bs_ref[gi])
            s = jnp.concatenate(parts, axis=0) if len(parts) > 1 else parts[0]
            gated = (u[:, gi * cg:(gi + 1) * cg] * s).astype(BF16)
            acc = acc + _dot(gated, wout_ref[gi * cg:(gi + 1) * cg, :])
        o_ref[rows, :] = acc

    n_sub = tm // sub
    project(0)
    for i in range(n_sub):
        if i + 1 < n_sub:
            project(i + 1)
        finish(i)


def _sg_mixer(x2d, g, w_in, b_in, ln_g, ln_b, w_s, b_s, w_out, chunk, emit_v, tm):
    n, d = x2d.shape
    sg = w_out.shape[0]
    reps = SG_CHUNK // chunk
    ws = jnp.tile(w_s[:, :chunk, :chunk], (1, reps, reps))
    bs = jnp.tile(b_s[:, :chunk], (1, reps)).reshape(SG_GROUPS, SG_CHUNK, 1)
    row_spec = pl.BlockSpec((tm, d), lambda i: (i, 0))
    out_shape = [jax.ShapeDtypeStruct((n, d), F32)]
    out_specs = [row_spec]
    if emit_v:
        out_shape.append(jax.ShapeDtypeStruct((n, sg), F32))
        out_specs.append(pl.BlockSpec((tm, sg), lambda i: (i, 0)))
    return pl.pallas_call(
        functools.partial(_sg_kernel, tm=tm, sub=min(tm, SG_SUB_ROWS), chunk=chunk),
        grid=(n // tm,),
        in_specs=[row_spec, _const_spec((1, d)), _const_spec((d, 2 * sg)), _const_spec((1, 2 * sg)),
                  _const_spec((1, sg)), _const_spec((1, sg)), _const_spec((SG_GROUPS, SG_CHUNK, SG_CHUNK)),
                  _const_spec((SG_GROUPS, SG_CHUNK, 1)), _const_spec((sg, d))],
        out_specs=out_specs,
        out_shape=out_shape,
        scratch_shapes=[pltpu.VMEM((2, min(tm, SG_SUB_ROWS), 2 * sg), F32)],
        compiler_params=_params(1),
        name="sg_mixer",
    )(x2d, g.reshape(1, d), w_in.astype(BF16), b_in.reshape(1, 2 * sg), ln_g.reshape(1, sg), ln_b.reshape(1, sg),
      ws, bs, w_out.astype(BF16))


def kernel(x_prompt, x_sample, state_conv, cache_k, cache_v, page_table, state_pool, state_ffn, norm_mix, norm_ffn, norm_final, cv_w_in, cv_b_in, cv_w_dw, cv_b_dw, cv_ln_g, cv_ln_b, cv_w_out, cv_b_out, da_w_qkv, da_lq1, da_lk1, da_lq2, da_lk2, da_norm_g, da_w_o, pl_w, pl_scale, sg_w_in, sg_b_in, sg_ln_g, sg_ln_b, sg_w_s, sg_b_s, sg_w_out, ff_w_gate, ff_w_up, ff_w_dw, ff_b_dw, ff_w_down):
    bp, seq, d = x_prompt.shape
    bs, dec_seq, _ = x_sample.shape
    depth, ffn_kw, d_ff = ff_w_dw.shape
    past_len = page_table.shape[1] * cache_k.shape[1]
    head_dim = d // (2 * DA_HEADS)
    kv_shape = (DA_HEADS, 2 * head_dim)

    wg_all, wu_all, wd_all = ff_w_gate.astype(BF16), ff_w_up.astype(BF16), ff_w_down.astype(BF16)

    def ffn(i, x, state, **kw):
        return _ffn(x, state, norm_ffn[i], wg_all, wu_all, ff_w_dw[i], ff_b_dw[i], wd_all, i, **kw)

    ffn_zero = jnp.zeros((bp, ffn_kw - 1, d_ff), F32)
    ffn_p, ffn_s = [], []

    cw = (norm_mix[0], cv_w_in, cv_b_in, cv_w_dw, cv_b_dw, cv_ln_g, cv_ln_b, cv_w_out, cv_b_out)
    xp, conv_p = _conv_mixer(x_prompt, jnp.zeros((bp,) + state_conv.shape[1:], F32), *cw)
    xs, conv_s = _conv_mixer(x_sample, state_conv, *cw)
    xp, st = ffn(0, xp, ffn_zero)
    ffn_p.append(st)
    xs, st = ffn(0, xs, state_ffn[0])
    ffn_s.append(st)

    lam_init = 0.8 - 0.6 * math.exp(-0.3 * 1)
    lqk = (da_lq1, da_lk1, da_lq2, da_lk2)
    scale = head_dim ** -0.5 * math.log2(math.e)
    qt_p, k_rows_p, v_rows_p, kh_p, vt_p = _qkv(xp.reshape(bp * seq, d), norm_mix[1], da_w_qkv, scale, seq=seq)
    o_p = _attn_prompt(qt_p, kh_p, vt_p, lqk, da_norm_g, lam_init, tq=min(seq // 2, ATTN_TILE))
    xp, st = ffn(1, xp, ffn_zero, proj=(o_p, da_w_o))
    ffn_p.append(st)
    q_s, k_rows_s, v_rows_s = _qkv(xs.reshape(bs * dec_seq, d), norm_mix[1], da_w_qkv, scale)
    rows_s = (bs, dec_seq) + kv_shape
    o_s = _attn_sample(q_s.reshape(rows_s), k_rows_s.reshape(rows_s), v_rows_s.reshape(rows_s),
                       cache_k, cache_v, page_table, lqk, da_norm_g, lam_init)
    xs, st = ffn(1, xs, state_ffn[1], proj=(o_s.reshape(bs, dec_seq, d), da_w_o))
    ffn_s.append(st)

    xp, pool_p = _pool_mixer(xp, jnp.zeros((bp,) + state_pool.shape[1:], F32), norm_mix[2], pl_w, pl_scale, 0)
    xs, pool_s = _pool_mixer(xs, state_pool, norm_mix[2], pl_w, pl_scale, past_len)
    xp, st = ffn(2, xp, ffn_zero)
    ffn_p.append(st)
    xs, st = ffn(2, xs, state_ffn[2])
    ffn_s.append(st)

    sw = (norm_mix[3], sg_w_in, sg_b_in, sg_ln_g, sg_ln_b, sg_w_s, sg_b_s, sg_w_out)
    assert seq % SG_CHUNK == 0 and past_len % SG_CHUNK == 0 and SG_CHUNK % dec_seq == 0
    (xp2,) = _sg_mixer(xp.reshape(bp * seq, d), *sw, chunk=SG_CHUNK, emit_v=False, tm=min(seq, 2 * SG_SUB_ROWS))
    xs2, sg_v = _sg_mixer(xs.reshape(bs * dec_seq, d), *sw, chunk=dec_seq, emit_v=True, tm=2 * LANES)
    y_prompt, st = ffn(3, xp2.reshape(bp, seq, d), ffn_zero, final_g=norm_final)
    ffn_p.append(st)
    y_sample, st = ffn(3, xs2.reshape(bs, dec_seq, d), state_ffn[3], final_g=norm_final)
    ffn_s.append(st)

    return (y_prompt, y_sample, conv_p, conv_s,
            k_rows_p.reshape((bp, seq) + kv_shape), v_rows_p.reshape((bp, seq) + kv_shape),
            k_rows_s.reshape((bs, dec_seq) + kv_shape), v_rows_s.reshape((bs, dec_seq) + kv_shape),
            pool_p, pool_s, sg_v.reshape(bs, dec_seq, -1), jnp.stack(ffn_p, axis=0), jnp.stack(ffn_s, axis=0))
```

```python
import functools
import math

import jax
import jax.numpy as jnp
from jax import lax
from jax.experimental import pallas as pl
from jax.experimental.pallas import tpu as pltpu

F32 = jnp.float32
BF16 = jnp.bfloat16
NORM_EPS = 1e-6
POOL_WINDOWS = (2, 4, 8, 16)
SG_CHUNK = 128
SG_GROUPS = 4
DA_HEADS = 8

SUBLANES = 8
LANES = 128
MXU_DIM = 256
VMEM_LIMIT_BYTES = 56 * 1024 * 1024
ROWS_PER_STEP = 512
SG_SUB_ROWS = 256
ATTN_TILE = 512
ATTN_TILES_PER_STEP = 4


def _round_up(n, m):
    return -(-n // m) * m


def _plan_bt(batch, seq, short_rows):
    if seq >= ROWS_PER_STEP:
        assert seq % ROWS_PER_STEP == 0
        return 1, ROWS_PER_STEP
    bb = max(1, min(batch, short_rows // seq))
    assert batch % bb == 0 and seq % SUBLANES == 0
    return bb, seq


def _const_spec(shape):
    nd = len(shape)
    return pl.BlockSpec(shape, lambda *_: (0,) * nd, pipeline_mode=pl.Buffered(1))


def _params(n_grid):
    return pltpu.CompilerParams(dimension_semantics=("arbitrary",) * n_grid,
                                vmem_limit_bytes=VMEM_LIMIT_BYTES)


def _rms_rows(x, g):
    return x * lax.rsqrt(jnp.mean(x * x, axis=-1, keepdims=True) + NORM_EPS) * g


def _layernorm_rows(x, g, b):
    mu = jnp.mean(x, axis=-1, keepdims=True)
    xc = x - mu
    return xc * lax.rsqrt(jnp.mean(xc * xc, axis=-1, keepdims=True) + NORM_EPS) * g + b


def _sigmoid(x):
    return 1.0 / (1.0 + jnp.exp(-x))


def _silu(x):
    return x * _sigmoid(x)


def _gelu_tanh(x):
    return x * (0.5 * (1.0 + jnp.tanh(math.sqrt(2.0 / math.pi) * (x + 0.044715 * (x * x * x)))))


def _idiv(x, n):
    assert n & (n - 1) == 0
    return x >> (n.bit_length() - 1)


def _imod(x, n):
    assert n & (n - 1) == 0
    return x & (n - 1)


def _dot(a, b):
    return jnp.dot(a, b, preferred_element_type=F32)


def _diff_lambda(lq1_ref, lk1_ref, lq2_ref, lk2_ref, lam_init):
    a = jnp.sum(lq1_ref[...] * lk1_ref[...], axis=-1, keepdims=True)
    b = jnp.sum(lq2_ref[...] * lk2_ref[...], axis=-1, keepdims=True)
    return jnp.exp(a) - jnp.exp(b) + lam_init


def _conv_mixer_kernel(x_ref, st_ref, g_ref, win_ref, bin_ref, wdw_ref, bdw_ref, lng_ref, lnb_ref,
                       wout_ref, bout_ref, o_ref, sto_ref, gp_ref, c_ref, sh_ref, *, bb, tt, kw, hp, sb, rb, cb):
    hist = kw - 1
    t = pl.program_id(1)
    d = x_ref.shape[-1]
    c = gp_ref.shape[-1]
    rows = bb * tt

    @pl.when(t == 0)
    def _():
        gp_ref[:, hp - hist:hp, :] = st_ref[...]

    x = x_ref[...].reshape(rows, d)
    h = _rms_rows(x, g_ref[...]).astype(BF16)
    ag = _dot(h, win_ref[...]) + bin_ref[...]
    glu = ag[:, :c] * _sigmoid(ag[:, c:])
    gp_ref[:, hp:hp + tt, :] = glu.reshape(bb, tt, c)

    base = hp - hist
    span = sh_ref.shape[2]
    for c0 in range(0, c, cb):
        for r in range(1, SUBLANES):
            sh_ref[r - 1] = gp_ref[:, r:r + span, c0:c0 + cb]
        for b0 in range(0, bb, sb):
            for r0 in range(0, tt, rb):
                acc = jnp.broadcast_to(bdw_ref[:, c0:c0 + cb].reshape(1, 1, cb), (sb, rb, cb))
                for k in range(kw):
                    a, r = divmod(base + k, SUBLANES)
                    lo = r0 + SUBLANES * a
                    if r == 0:
                        win = gp_ref[b0:b0 + sb, lo:lo + rb, c0:c0 + cb]
                    else:
                        win = sh_ref[r - 1, b0:b0 + sb, lo:lo + rb, :]
                    acc = acc + win * wdw_ref[k:k + 1, c0:c0 + cb].reshape(1, 1, cb)
                c_ref[b0:b0 + sb, r0:r0 + rb, c0:c0 + cb] = acc

    conv = c_ref[...].reshape(rows, c)
    act = _silu(_layernorm_rows(conv, lng_ref[...], lnb_ref[...])).astype(BF16)
    m = _dot(act, wout_ref[...]) + bout_ref[...]
    o_ref[...] = (x + m).reshape(bb, tt, d)

    new_hist = gp_ref[:, tt + hp - hist:tt + hp, :]
    gp_ref[:, hp - hist:hp, :] = new_hist

    @pl.when(t == pl.num_programs(1) - 1)
    def _():
        sto_ref[...] = new_hist


def _conv_mixer(x, state, g, w_in, b_in, w_dw, b_dw, ln_g, ln_b, w_out, b_out):
    batch, seq, d = x.shape
    kw, c = w_dw.shape
    hist = kw - 1
    hp = _round_up(hist, SUBLANES)
    bb, tt = _plan_bt(batch, seq, short_rows=ROWS_PER_STEP // 4)
    cb = 2 * LANES
    rb = min(tt, LANES)
    sb = max(1, min(bb, LANES // rb))
    kern = functools.partial(_conv_mixer_kernel, bb=bb, tt=tt, kw=kw, hp=hp, sb=sb, rb=rb, cb=cb)
    x_spec = pl.BlockSpec((bb, tt, d), lambda b, t: (b, t, 0))
    st_spec = pl.BlockSpec((bb, hist, c), lambda b, t: (b, 0, 0))
    return pl.pallas_call(
        kern,
        grid=(batch // bb, seq // tt),
        in_specs=[x_spec, st_spec, _const_spec((1, d)), _const_spec((d, 2 * c)), _const_spec((1, 2 * c)),
                  _const_spec((kw, c)), _const_spec((1, c)), _const_spec((1, c)), _const_spec((1, c)),
                  _const_spec((c, d)), _const_spec((1, d))],
        out_specs=[x_spec, st_spec],
        out_shape=[jax.ShapeDtypeStruct((batch, seq, d), F32), jax.ShapeDtypeStruct((batch, hist, c), F32)],
        scratch_shapes=[pltpu.VMEM((bb, hp + tt, c), F32), pltpu.VMEM((bb, tt, c), F32),
                        pltpu.VMEM((SUBLANES - 1, bb, hp + tt - SUBLANES, cb), F32)],
        compiler_params=_params(2),
        name="conv_mixer",
    )(x, state, g.reshape(1, d), w_in.astype(BF16), b_in.reshape(1, 2 * c), w_dw, b_dw.reshape(1, c),
      ln_g.reshape(1, c), ln_b.reshape(1, c), w_out.astype(BF16), b_out.reshape(1, d))


def _ffn_kernel(*refs, bb, tt, proj, final, chunks, hp):
    refs = list(refs)
    x_ref = refs.pop(0)
    if proj:
        m_ref = refs.pop(0)
        wp_ref = refs.pop(0)
    st_ref, g_ref, wg_ref, wu_ref, wdw_ref, bdw_ref, wd_ref = refs[:7]
    refs = refs[7:]
    if final:
        gf_ref = refs.pop(0)
    o_ref, sto_ref, gpad_ref, carry_ref = refs
    hist = wdw_ref.shape[0] - 1
    t = pl.program_id(1)
    d = x_ref.shape[-1]
    rows = bb * tt

    @pl.when(t == 0)
    def _():
        carry_ref[...] = st_ref[...]

    x = x_ref[...].reshape(rows, d)
    if proj:
        x = x + _dot(m_ref[...].reshape(rows, m_ref.shape[-1]).astype(BF16), wp_ref[...])
    h = _rms_rows(x, g_ref[...]).astype(BF16)
    acc = x
    base = hp - hist
    for c0, fc in chunks:
        gate = _dot(h, wg_ref[:, c0:c0 + fc]).reshape(bb, tt, fc)
        up = _dot(h, wu_ref[:, c0:c0 + fc])
        gpad_ref[:, base:hp, :fc] = carry_ref[:, :, c0:c0 + fc]
        gpad_ref[:, hp:hp + tt, :fc] = gate
        gc = gate * wdw_ref[hist:hist + 1, c0:c0 + fc].reshape(1, 1, fc) + bdw_ref[:, c0:c0 + fc].reshape(1, 1, fc)
        for k in range(hist):
            gc = gc + gpad_ref[:, base + k:base + k + tt, :fc] * wdw_ref[k:k + 1, c0:c0 + fc].reshape(1, 1, fc)
        carry_ref[:, :, c0:c0 + fc] = gpad_ref[:, tt + base:tt + hp, :fc]
        act = (_silu(gc).reshape(rows, fc) * up).astype(BF16)
        acc = acc + _dot(act, wd_ref[c0:c0 + fc, :])
    if final:
        acc = _rms_rows(acc, gf_ref[...])
    o_ref[...] = acc.reshape(bb, tt, d)
    sto_ref[...] = carry_ref[...]


def _ffn(x, state, g, w_gate, w_up, w_dw, b_dw, w_down, layer, proj=None, final_g=None):
    batch, seq, d = x.shape
    kw, f = w_dw.shape

    def layer_spec(shape):
        return pl.BlockSpec((None,) + shape, lambda *_: (layer, 0, 0), pipeline_mode=pl.Buffered(1))

    hist = kw - 1
    hp = _round_up(hist, SUBLANES)
    bb, tt = _plan_bt(batch, seq, short_rows=ROWS_PER_STEP // 2)
    n_tiles = f // MXU_DIM
    assert f % MXU_DIM == 0
    half = (n_tiles + 1) // 2 * MXU_DIM
    chunks = ((0, half), (half, f - half)) if f > half else ((0, f),)
    kern = functools.partial(_ffn_kernel, bb=bb, tt=tt, proj=proj is not None, final=final_g is not None,
                             chunks=chunks, hp=hp)
    x_spec = pl.BlockSpec((bb, tt, d), lambda b, t: (b, t, 0))
    st_spec = pl.BlockSpec((bb, hist, f), lambda b, t: (b, 0, 0))
    args, specs = [x], [x_spec]
    if proj is not None:
        m, w_proj = proj
        dm = m.shape[-1]
        args += [m, w_proj.astype(BF16)]
        specs += [pl.BlockSpec((bb, tt, dm), lambda b, t: (b, t, 0)), _const_spec((dm, d))]
    args += [state, g.reshape(1, d), w_gate, w_up, w_dw, b_dw.reshape(1, f), w_down]
    specs += [st_spec, _const_spec((1, d)), layer_spec((d, f)), layer_spec((d, f)), _const_spec((kw, f)),
              _const_spec((1, f)), layer_spec((f, d))]
    if final_g is not None:
        args.append(final_g.reshape(1, d))
        specs.append(_const_spec((1, d)))
    return pl.pallas_call(
        kern,
        grid=(batch // bb, seq // tt),
        in_specs=specs,
        out_specs=[x_spec, st_spec],
        out_shape=[jax.ShapeDtypeStruct((batch, seq, d), F32), jax.ShapeDtypeStruct((batch, hist, f), F32)],
        scratch_shapes=[pltpu.VMEM((bb, hp + tt, chunks[0][1]), F32), pltpu.VMEM((bb, hist, f), F32)],
        compiler_params=_params(2),
        name="conv_ffn",
    )(*args)


def _qkv_kernel(x_ref, g_ref, w_ref, q_ref, k_ref, v_ref, *maybe_heads, scale):
    dq = k_ref.shape[-1]
    h = _rms_rows(x_ref[...], g_ref[...]).astype(BF16)
    qkv = _dot(h, w_ref[...])
    q = qkv[:, :dq] * scale
    k = qkv[:, dq:2 * dq]
    v = qkv[:, 2 * dq:]
    k_ref[...] = k
    v_ref[...] = v
    if maybe_heads:
        kh_ref, vt_ref = maybe_heads
        q_ref[...] = q.T.astype(BF16)
        vt_ref[...] = v.T.astype(BF16)
        e = kh_ref.shape[-1]
        for hd in range(kh_ref.shape[0]):
            kh_ref[hd] = k[:, hd * e:(hd + 1) * e].astype(BF16)
    else:
        q_ref[...] = q


def _qkv(x2d, g, w_qkv, scale, seq=None):
    n, d = x2d.shape
    dq = w_qkv.shape[1] // 3
    tm = min(n, ROWS_PER_STEP)
    row_spec = pl.BlockSpec((tm, dq), lambda i: (i, 0))
    rows_f32 = jax.ShapeDtypeStruct((n, dq), F32)
    if seq is None:
        out_shape = [rows_f32] * 3
        out_specs = [row_spec] * 3
    else:
        assert seq % tm == 0
        per_seq = seq // tm
        e = dq // DA_HEADS
        t_shape = jax.ShapeDtypeStruct((n // seq, dq, seq), BF16)
        t_spec = pl.BlockSpec((None, dq, tm), lambda i: (i // per_seq, 0, i % per_seq))
        out_shape = [t_shape, rows_f32, rows_f32, jax.ShapeDtypeStruct((n // seq, DA_HEADS, seq, e), BF16), t_shape]
        out_specs = [t_spec, row_spec, row_spec,
                     pl.BlockSpec((None, DA_HEADS, tm, e), lambda i: (i // per_seq, 0, i % per_seq, 0)), t_spec]
    return pl.pallas_call(
        functools.partial(_qkv_kernel, scale=scale),
        grid=(n // tm,),
        in_specs=[pl.BlockSpec((tm, d), lambda i: (i, 0)), _const_spec((1, d)), _const_spec((d, 3 * dq))],
        out_specs=out_specs,
        out_shape=out_shape,
        compiler_params=_params(1),
        name="qkv",
    )(x2d, g.reshape(1, d), w_qkv.astype(BF16))


def _subln(o, ng, lam_init):
    return o * lax.rsqrt(jnp.mean(o * o, axis=-1, keepdims=True) + NORM_EPS) * ng * (1.0 - lam_init)


def _attn_prompt_kernel(lq1_ref, lk1_ref, lq2_ref, lk2_ref, ng_ref, qt_ref, k_ref, vt_ref, o_ref, sa_ref, sb_ref,
                        *, tq, nt, lam_init):
    j = pl.program_id(2)
    e = qt_ref.shape[0]
    w = 2 * tq
    qt = qt_ref[...].astype(F32)
    row = lax.broadcasted_iota(jnp.int32, qt.shape, 0)
    lo = jnp.where(row < e // 2, qt, 0.0)
    hi = jnp.where(row >= e // 2, qt, 0.0)
    qq = jnp.concatenate([half[:, g * tq:(g + 1) * tq] for g in range(nt) for half in (lo, hi)],
                         axis=1).astype(BF16)

    def scores(c, dst_ref, first_col):
        start = pl.multiple_of(c * tq, tq)
        dst_ref[:, first_col:] = _dot(k_ref[pl.ds(start, tq), :], qq[:, first_col:])

    def absorb(src_ref, c, carry, tile, own_chunk):
        m, l, acc = carry
        start = pl.multiple_of(c * tq, tq)
        s = src_ref[:, tile * w:(tile + 1) * w]
        if own_chunk:
            key = lax.broadcasted_iota(jnp.int32, s.shape, 0)
            col = lax.broadcasted_iota(jnp.int32, s.shape, 1)
            s = jnp.where(key <= jnp.where(col >= tq, col - tq, col), s, -jnp.inf)
        m_new = jnp.maximum(m, jnp.max(s, axis=0, keepdims=True))
        p = jnp.exp2(s - m_new)
        alpha = jnp.exp2(m - m_new)
        l = alpha * l + jnp.sum(p, axis=0, keepdims=True)
        return m_new, l, alpha * acc + _dot(vt_ref[:, pl.ds(start, tq)], p.astype(BF16))

    init = (jnp.full((1, w), -jnp.inf, F32), jnp.zeros((1, w), F32), jnp.zeros((e, w), F32))
    scores(0, sa_ref, 0)

    def two_chunks(i, carries):
        scores(2 * i + 1, sb_ref, 0)
        carries = tuple(absorb(sa_ref, 2 * i, c, g, False) for g, c in enumerate(carries))
        scores(2 * i + 2, sa_ref, 0)
        return tuple(absorb(sb_ref, 2 * i + 1, c, g, False) for g, c in enumerate(carries))

    assert nt % 2 == 0
    first = nt * j
    carries = list(lax.fori_loop(0, (nt // 2) * j, two_chunks, (init,) * nt))
    bufs = (sa_ref, sb_ref)
    for d in range(nt):
        if d + 1 < nt:
            scores(first + d + 1, bufs[(d + 1) % 2], (d + 1) * w)
        for g in range(d, nt):
            carries[g] = absorb(bufs[d % 2], first + d, carries[g], g, g == d)

    lam = _diff_lambda(lq1_ref, lk1_ref, lq2_ref, lk2_ref, lam_init)
    for tile, (_, l, acc) in enumerate(carries):
        o = acc * (1.0 / l)
        o = (o[:, :tq] - lam * o[:, tq:]).T
        o_ref[tile * tq:(tile + 1) * tq, :] = _subln(o, ng_ref[...], lam_init).astype(o_ref.dtype)


def _attn_prompt(qt, k, vt, lqk, norm_g, lam_init, tq, nt):
    batch, dq, seq = qt.shape
    e = dq // DA_HEADS
    dl = lqk[0].shape[0]
    assert seq % (nt * tq) == 0
    qt_spec = pl.BlockSpec((None, e, nt * tq), lambda b, h, i: (b, h, i))
    k_spec = pl.BlockSpec((None, None, seq, e), lambda b, h, i: (b, h, 0, 0))
    vt_spec = pl.BlockSpec((None, e, seq), lambda b, h, i: (b, h, 0))
    return pl.pallas_call(
        functools.partial(_attn_prompt_kernel, tq=tq, nt=nt, lam_init=lam_init),
        grid=(batch, DA_HEADS, seq // (nt * tq)),
        in_specs=[_const_spec((1, dl))] * 4 + [pl.BlockSpec((1, e), lambda b, h, i: (0, h)), qt_spec, k_spec, vt_spec],
        out_specs=pl.BlockSpec((None, nt * tq, e), lambda b, h, i: (b, i, h)),
        out_shape=jax.ShapeDtypeStruct((batch, seq, dq), BF16),
        scratch_shapes=[pltpu.VMEM((tq, 2 * nt * tq), F32), pltpu.VMEM((tq, 2 * nt * tq), F32)],
        compiler_params=_params(3),
        name="attn_prompt",
    )(*[a.reshape(1, dl) for a in lqk], norm_g.reshape(1, dq), qt, k, vt)


def _attn_sample_kernel(pt_ref, lq1_ref, lk1_ref, lq2_ref, lk2_ref, ng_ref, q_ref, kn_ref, vn_ref, *rest,
                        n_pages, lam_init):
    del pt_ref
    k_pages = rest[:n_pages]
    v_pages = rest[n_pages:2 * n_pages]
    o_ref, s_ref = rest[2 * n_pages:]
    t, nh, e = q_ref.shape
    page = k_pages[0].shape[0]
    rows = t * nh
    cols = page * nh
    nt = (((1,), (1,)), ((), ()))

    q2 = q_ref[...].reshape(rows, e)
    lane = lax.broadcasted_iota(jnp.int32, q2.shape, 1)
    wq = jnp.concatenate([jnp.where(lane < e // 2, q2, 0.0), jnp.where(lane >= e // 2, q2, 0.0)], axis=0).astype(BF16)

    r = lax.broadcasted_iota(jnp.int32, (2 * rows, cols), 0)
    c = lax.broadcasted_iota(jnp.int32, (2 * rows, cols), 1)
    same_head = _imod(c - r, nh) == 0

    rn = lax.broadcasted_iota(jnp.int32, (2 * rows, rows), 0)
    cn = lax.broadcasted_iota(jnp.int32, (2 * rows, rows), 1)
    keep_new = (_imod(cn - rn, nh) == 0) & (_idiv(cn, nh) <= _idiv(_imod(rn, rows), nh))
    s_new = lax.dot_general(wq, kn_ref[...].reshape(rows, e).astype(BF16), nt, preferred_element_type=F32)
    s_new = jnp.where(keep_new, s_new, -jnp.inf)

    mx = None
    for p in range(n_pages):
        k2 = k_pages[p][...].reshape(cols, e).astype(BF16)
        s = lax.dot_general(wq, k2, nt, preferred_element_type=F32)
        s_ref[:, p * cols:(p + 1) * cols] = s
        mx = s if mx is None else jnp.maximum(mx, s)
    mx = jnp.where(same_head, mx, -jnp.inf)
    m = jnp.maximum(jnp.max(mx, axis=-1, keepdims=True), jnp.max(s_new, axis=-1, keepdims=True))

    shift = jnp.where(same_head, -m, -jnp.inf)
    p_new = jnp.exp2(s_new - m)
    acc = _dot(p_new.astype(BF16), vn_ref[...].reshape(rows, e).astype(BF16))
    tot = None
    for p in range(n_pages):
        pe = jnp.exp2(s_ref[:, p * cols:(p + 1) * cols] + shift)
        tot = pe if tot is None else tot + pe
        acc = acc + _dot(pe.astype(BF16), v_pages[p][...].reshape(cols, e).astype(BF16))
    l = jnp.sum(tot, axis=-1, keepdims=True) + jnp.sum(p_new, axis=-1, keepdims=True)

    o = acc * (1.0 / l)
    lam = _diff_lambda(lq1_ref, lk1_ref, lq2_ref, lk2_ref, lam_init)
    o = o[:rows] - lam * o[rows:]
    ng = jnp.concatenate([ng_ref[...]] * t, axis=0)
    o_ref[...] = _subln(o, ng, lam_init).reshape(t, nh, e)


def _attn_sample(q, k_new, v_new, cache_k, cache_v, page_table, lqk, norm_g, lam_init):
    batch, t, nh, e = q.shape
    page = cache_k.shape[1]
    n_pages = page_table.shape[1]
    dl = lqk[0].shape[0]
    row_spec = pl.BlockSpec((None, t, nh, e), lambda b, pt: (b, 0, 0, 0))

    def page_spec(p):
        return pl.BlockSpec((None, page, nh, e), lambda b, pt: (pt[b, p], 0, 0, 0))

    page_specs = [page_spec(p) for p in range(n_pages)]
    grid_spec = pltpu.PrefetchScalarGridSpec(
        num_scalar_prefetch=1,
        grid=(batch,),
        in_specs=[_const_spec((1, dl))] * 4 + [_const_spec((nh, e)), row_spec, row_spec, row_spec]
        + page_specs + page_specs,
        out_specs=row_spec,
        scratch_shapes=[pltpu.VMEM((2 * t * nh, n_pages * page * nh), F32)],
    )
    return pl.pallas_call(
        functools.partial(_attn_sample_kernel, n_pages=n_pages, lam_init=lam_init),
        grid_spec=grid_spec,
        out_shape=jax.ShapeDtypeStruct((batch, t, nh, e), F32),
        compiler_params=_params(1),
        name="attn_sample",
    )(page_table, *[a.reshape(1, dl) for a in lqk], norm_g.reshape(nh, e), q, k_new, v_new,
      *([cache_k] * n_pages), *([cache_v] * n_pages))


def _pool_kernel(x_ref, st_ref, g_ref, w_ref, sc_ref, o_ref, sto_ref, hp_ref, lv_ref, *, bb, tt, windows, pos0,
                 hist, hp):
    t = pl.program_id(1)
    d = x_ref.shape[-1]
    cg = d // len(windows)
    rows = bb * tt
    lead = SUBLANES

    @pl.when(t == 0)
    def _():
        hp_ref[:, :hp - hist, :] = jnp.zeros((bb, hp - hist, d), F32)
        hp_ref[:, hp - hist:hp, :] = st_ref[...]
        lv_ref[:, :, :lead, :] = jnp.zeros((2, bb, lead, cg), F32)

    x = x_ref[...].reshape(rows, d)
    h = _rms_rows(x, g_ref[...]).reshape(bb, tt, d)
    hp_ref[:, hp:hp + tt, :] = h
    pos = pos0 + t * tt + lax.broadcasted_iota(jnp.int32, (1, tt, 1), 1)
    ys = []
    for gi, win in enumerate(windows):
        assert win & (win - 1) == 0 and win <= hp - lead
        c0 = gi * cg
        cur = h[:, :, c0:c0 + cg]

        def level_rows(level, lo, hi):
            if level == 0:
                return hp_ref[:, lo:hi, c0:c0 + cg]
            return lv_ref[level % 2, :, lo:hi, :]

        n_levels = win.bit_length() - 1
        for level in range(n_levels - 1):
            w = 1 << level
            lv_ref[(level + 1) % 2, :, lead:hp + tt, :] = (level_rows(level, lead, hp + tt)
                                                            + level_rows(level, lead - w, hp + tt - w))
        w = win // 2
        tot = level_rows(n_levels - 1, hp, hp + tt) + level_rows(n_levels - 1, hp - w, hp + tt - w)
        cnt = jnp.minimum(win, pos + 1).astype(F32)
        pooled = (tot / cnt - cur).reshape(rows, cg).astype(BF16)
        ys.append(_dot(pooled, w_ref[gi]))
    y = jnp.concatenate(ys, axis=-1) * sc_ref[...]
    o_ref[...] = (x + y).reshape(bb, tt, d)

    new_hist = hp_ref[:, tt + hp - hist:tt + hp, :]
    hp_ref[:, hp - hist:hp, :] = new_hist

    @pl.when(t == pl.num_programs(1) - 1)
    def _():
        sto_ref[...] = new_hist


def _pool_mixer(x, state, g, w_grp, scale, pos0):
    batch, seq, d = x.shape
    hist = max(POOL_WINDOWS) - 1
    hp = _round_up(hist, SUBLANES) + SUBLANES
    n_grp, cg, _ = w_grp.shape
    bb, tt = _plan_bt(batch, seq, short_rows=ROWS_PER_STEP // 2)
    x_spec = pl.BlockSpec((bb, tt, d), lambda b, t: (b, t, 0))
    st_spec = pl.BlockSpec((bb, hist, d), lambda b, t: (b, 0, 0))
    return pl.pallas_call(
        functools.partial(_pool_kernel, bb=bb, tt=tt, windows=POOL_WINDOWS, pos0=pos0, hist=hist, hp=hp),
        grid=(batch // bb, seq // tt),
        in_specs=[x_spec, st_spec, _const_spec((1, d)), _const_spec((n_grp, cg, cg)), _const_spec((1, d))],
        out_specs=[x_spec, st_spec],
        out_shape=[jax.ShapeDtypeStruct((batch, seq, d), F32), jax.ShapeDtypeStruct((batch, hist, d), F32)],
        scratch_shapes=[pltpu.VMEM((bb, hp + tt, d), F32), pltpu.VMEM((2, bb, hp + tt, cg), F32)],
        compiler_params=_params(2),
        name="pool_mixer",
    )(x, state, g.reshape(1, d), w_grp.astype(BF16), scale.reshape(1, d))


def _sg_kernel(x_ref, g_ref, win_ref, bin_ref, lng_ref, lnb_ref, ws_ref, bs_ref, wout_ref, o_ref, *rest,
               tm, sub, chunk):
    *maybe_v, z_ref = rest
    sg = wout_ref.shape[0]
    cg = sg // SG_GROUPS
    r = lax.broadcasted_iota(jnp.int32, (SG_CHUNK, SG_CHUNK), 0)
    c = lax.broadcasted_iota(jnp.int32, (SG_CHUNK, SG_CHUNK), 1)
    keep = (_idiv(r, chunk) == _idiv(c, chunk)) & (c <= r)
    wss = [jnp.where(keep, ws_ref[gi], 0.0).astype(BF16) for gi in range(SG_GROUPS)]

    def project(i):
        h = _rms_rows(x_ref[i * sub:(i + 1) * sub, :], g_ref[...]).astype(BF16)
        z_ref[i % 2] = _dot(h, win_ref[...])

    def finish(i):
        rows = slice(i * sub, (i + 1) * sub)
        z = _gelu_tanh(z_ref[i % 2] + bin_ref[...])
        u = z[:, :sg]
        v = _layernorm_rows(z[:, sg:], lng_ref[...], lnb_ref[...])
        if maybe_v:
            maybe_v[0][rows, :] = v
        vb = v.astype(BF16)
        acc = x_ref[rows, :]
        for gi in range(SG_GROUPS):
            parts = []
            for r0 in range(0, sub, SG_CHUNK):
                parts.append(_dot(wss[gi], vb[r0:r0 + SG_CHUNK, gi * cg:(gi + 1) * cg]) + bs_ref[gi])
            s = jnp.concatenate(parts, axis=0) if len(parts) > 1 else parts[0]
            gated = (u[:, gi * cg:(gi + 1) * cg] * s).astype(BF16)
            acc = acc + _dot(gated, wout_ref[gi * cg:(gi + 1) * cg, :])
        o_ref[rows, :] = acc

    n_sub = tm // sub
    project(0)
    for i in range(n_sub):
        if i + 1 < n_sub:
            project(i + 1)
        finish(i)


def _sg_mixer(x2d, g, w_in, b_in, ln_g, ln_b, w_s, b_s, w_out, chunk, emit_v, tm):
    n, d = x2d.shape
    sg = w_out.shape[0]
    reps = SG_CHUNK // chunk
    ws = jnp.tile(w_s[:, :chunk, :chunk], (1, reps, reps))
    bs = jnp.tile(b_s[:, :chunk], (1, reps)).reshape(SG_GROUPS, SG_CHUNK, 1)
    row_spec = pl.BlockSpec((tm, d), lambda i: (i, 0))
    out_shape = [jax.ShapeDtypeStruct((n, d), F32)]
    out_specs = [row_spec]
    if emit_v:
        out_shape.append(jax.ShapeDtypeStruct((n, sg), F32))
        out_specs.append(pl.BlockSpec((tm, sg), lambda i: (i, 0)))
    return pl.pallas_call(
        functools.partial(_sg_kernel, tm=tm, sub=min(tm, SG_SUB_ROWS), chunk=chunk),
        grid=(n // tm,),
        in_specs=[row_spec, _const_spec((1, d)), _const_spec((d, 2 * sg)), _const_spec((1, 2 * sg)),
                  _const_spec((1, sg)), _const_spec((1, sg)), _const_spec((SG_GROUPS, SG_CHUNK, SG_CHUNK)),
                  _const_spec((SG_GROUPS, SG_CHUNK, 1)), _const_spec((sg, d))],
        out_specs=out_specs,
        out_shape=out_shape,
        scratch_shapes=[pltpu.VMEM((2, min(tm, SG_SUB_ROWS), 2 * sg), F32)],
        compiler_params=_params(1),
        name="sg_mixer",
    )(x2d, g.reshape(1, d), w_in.astype(BF16), b_in.reshape(1, 2 * sg), ln_g.reshape(1, sg), ln_b.reshape(1, sg),
      ws, bs, w_out.astype(BF16))


def kernel(x_prompt, x_sample, state_conv, cache_k, cache_v, page_table, state_pool, state_ffn, norm_mix, norm_ffn, norm_final, cv_w_in, cv_b_in, cv_w_dw, cv_b_dw, cv_ln_g, cv_ln_b, cv_w_out, cv_b_out, da_w_qkv, da_lq1, da_lk1, da_lq2, da_lk2, da_norm_g, da_w_o, pl_w, pl_scale, sg_w_in, sg_b_in, sg_ln_g, sg_ln_b, sg_w_s, sg_b_s, sg_w_out, ff_w_gate, ff_w_up, ff_w_dw, ff_b_dw, ff_w_down):
    bp, seq, d = x_prompt.shape
    bs, dec_seq, _ = x_sample.shape
    depth, ffn_kw, d_ff = ff_w_dw.shape
    past_len = page_table.shape[1] * cache_k.shape[1]
    head_dim = d // (2 * DA_HEADS)
    kv_shape = (DA_HEADS, 2 * head_dim)

    wg_all, wu_all, wd_all = ff_w_gate.astype(BF16), ff_w_up.astype(BF16), ff_w_down.astype(BF16)

    def ffn(i, x, state, **kw):
        return _ffn(x, state, norm_ffn[i], wg_all, wu_all, ff_w_dw[i], ff_b_dw[i], wd_all, i, **kw)

    ffn_zero = jnp.zeros((bp, ffn_kw - 1, d_ff), F32)
    ffn_p, ffn_s = [], []

    cw = (norm_mix[0], cv_w_in, cv_b_in, cv_w_dw, cv_b_dw, cv_ln_g, cv_ln_b, cv_w_out, cv_b_out)
    xp, conv_p = _conv_mixer(x_prompt, jnp.zeros((bp,) + state_conv.shape[1:], F32), *cw)
    xs, conv_s = _conv_mixer(x_sample, state_conv, *cw)
    xp, st = ffn(0, xp, ffn_zero)
    ffn_p.append(st)
    xs, st = ffn(0, xs, state_ffn[0])
    ffn_s.append(st)

    lam_init = 0.8 - 0.6 * math.exp(-0.3 * 1)
    lqk = (da_lq1, da_lk1, da_lq2, da_lk2)
    scale = head_dim ** -0.5 * math.log2(math.e)
    qt_p, k_rows_p, v_rows_p, kh_p, vt_p = _qkv(xp.reshape(bp * seq, d), norm_mix[1], da_w_qkv, scale, seq=seq)
    o_p = _attn_prompt(qt_p, kh_p, vt_p, lqk, da_norm_g, lam_init, tq=min(seq // ATTN_TILES_PER_STEP, ATTN_TILE),
                       nt=ATTN_TILES_PER_STEP)
    xp, st = ffn(1, xp, ffn_zero, proj=(o_p, da_w_o))
    ffn_p.append(st)
    q_s, k_rows_s, v_rows_s = _qkv(xs.reshape(bs * dec_seq, d), norm_mix[1], da_w_qkv, scale)
    rows_s = (bs, dec_seq) + kv_shape
    o_s = _attn_sample(q_s.reshape(rows_s), k_rows_s.reshape(rows_s), v_rows_s.reshape(rows_s),
                       cache_k, cache_v, page_table, lqk, da_norm_g, lam_init)
    xs, st = ffn(1, xs, state_ffn[1], proj=(o_s.reshape(bs, dec_seq, d), da_w_o))
    ffn_s.append(st)

    xp, pool_p = _pool_mixer(xp, jnp.zeros((bp,) + state_pool.shape[1:], F32), norm_mix[2], pl_w, pl_scale, 0)
    xs, pool_s = _pool_mixer(xs, state_pool, norm_mix[2], pl_w, pl_scale, past_len)
    xp, st = ffn(2, xp, ffn_zero)
    ffn_p.append(st)
    xs, st = ffn(2, xs, state_ffn[2])
    ffn_s.append(st)

    sw = (norm_mix[3], sg_w_in, sg_b_in, sg_ln_g, sg_ln_b, sg_w_s, sg_b_s, sg_w_out)
    assert seq % SG_CHUNK == 0 and past_len % SG_CHUNK == 0 and SG_CHUNK % dec_seq == 0
    (xp2,) = _sg_mixer(xp.reshape(bp * seq, d), *sw, chunk=SG_CHUNK, emit_v=False, tm=min(seq, 2 * SG_SUB_ROWS))
    xs2, sg_v = _sg_mixer(xs.reshape(bs * dec_seq, d), *sw, chunk=dec_seq, emit_v=True, tm=2 * LANES)
    y_prompt, st = ffn(3, xp2.reshape(bp, seq, d), ffn_zero, final_g=norm_final)
    ffn_p.append(st)
    y_sample, st = ffn(3, xs2.reshape(bs, dec_seq, d), state_ffn[3], final_g=norm_final)
    ffn_s.append(st)

    return (y_prompt, y_sample, conv_p, conv_s,
            k_rows_p.reshape((bp, seq) + kv_shape), v_rows_p.reshape((bp, seq) + kv_shape),
            k_rows_s.reshape((bs, dec_seq) + kv_shape), v_rows_s.reshape((bs, dec_seq) + kv_shape),
            pool_p, pool_s, sg_v.reshape(bs, dec_seq, -1), jnp.stack(ffn_p, axis=0), jnp.stack(ffn_s, axis=0))
```

```python
import functools
import math

import jax
import jax.numpy as jnp
from jax import lax
from jax.experimental import pallas as pl
from jax.experimental.pallas import tpu as pltpu

F32 = jnp.float32
BF16 = jnp.bfloat16
NORM_EPS = 1e-6
POOL_WINDOWS = (2, 4, 8, 16)
SG_CHUNK = 128
SG_GROUPS = 4
DA_HEADS = 8

SUBLANES = 8
LANES = 128
MXU_DIM = 256
VMEM_LIMIT_BYTES = 56 * 1024 * 1024
ROWS_PER_STEP = 512
SG_SUB_ROWS = 256
ATTN_TILE = 512
ATTN_TILES_PER_STEP = 8


def _round_up(n, m):
    return -(-n // m) * m


def _plan_bt(batch, seq, short_rows):
    if seq >= ROWS_PER_STEP:
        assert seq % ROWS_PER_STEP == 0
        return 1, ROWS_PER_STEP
    bb = max(1, min(batch, short_rows // seq))
    assert batch % bb == 0 and seq % SUBLANES == 0
    return bb, seq


def _const_spec(shape):
    nd = len(shape)
    return pl.BlockSpec(shape, lambda *_: (0,) * nd, pipeline_mode=pl.Buffered(1))


def _params(n_grid):
    return pltpu.CompilerParams(dimension_semantics=("arbitrary",) * n_grid,
                                vmem_limit_bytes=VMEM_LIMIT_BYTES)


def _rms_rows(x, g):
    return x * lax.rsqrt(jnp.mean(x * x, axis=-1, keepdims=True) + NORM_EPS) * g


def _layernorm_rows(x, g, b):
    mu = jnp.mean(x, axis=-1, keepdims=True)
    xc = x - mu
    return xc * lax.rsqrt(jnp.mean(xc * xc, axis=-1, keepdims=True) + NORM_EPS) * g + b


def _sigmoid(x):
    return 1.0 / (1.0 + jnp.exp(-x))


def _silu(x):
    return x * _sigmoid(x)


def _gelu_tanh(x):
    return x * (0.5 * (1.0 + jnp.tanh(math.sqrt(2.0 / math.pi) * (x + 0.044715 * (x * x * x)))))


def _idiv(x, n):
    assert n & (n - 1) == 0
    return x >> (n.bit_length() - 1)


def _imod(x, n):
    assert n & (n - 1) == 0
    return x & (n - 1)


def _dot(a, b):
    return jnp.dot(a, b, preferred_element_type=F32)


def _diff_lambda(lq1_ref, lk1_ref, lq2_ref, lk2_ref, lam_init):
    a = jnp.sum(lq1_ref[...] * lk1_ref[...], axis=-1, keepdims=True)
    b = jnp.sum(lq2_ref[...] * lk2_ref[...], axis=-1, keepdims=True)
    return jnp.exp(a) - jnp.exp(b) + lam_init


def _conv_mixer_kernel(x_ref, st_ref, g_ref, win_ref, bin_ref, wdw_ref, bdw_ref, lng_ref, lnb_ref,
                       wout_ref, bout_ref, o_ref, sto_ref, gp_ref, c_ref, sh_ref, *, bb, tt, kw, hp, sb, rb, cb):
    hist = kw - 1
    t = pl.program_id(1)
    d = x_ref.shape[-1]
    c = gp_ref.shape[-1]
    rows = bb * tt

    @pl.when(t == 0)
    def _():
        gp_ref[:, hp - hist:hp, :] = st_ref[...]

    x = x_ref[...].reshape(rows, d)
    h = _rms_rows(x, g_ref[...]).astype(BF16)
    ag = _dot(h, win_ref[...]) + bin_ref[...]
    glu = ag[:, :c] * _sigmoid(ag[:, c:])
    gp_ref[:, hp:hp + tt, :] = glu.reshape(bb, tt, c)

    base = hp - hist
    span = sh_ref.shape[2]
    for c0 in range(0, c, cb):
        for r in range(1, SUBLANES):
            sh_ref[r - 1] = gp_ref[:, r:r + span, c0:c0 + cb]
        for b0 in range(0, bb, sb):
            for r0 in range(0, tt, rb):
                acc = jnp.broadcast_to(bdw_ref[:, c0:c0 + cb].reshape(1, 1, cb), (sb, rb, cb))
                for k in range(kw):
                    a, r = divmod(base + k, SUBLANES)
                    lo = r0 + SUBLANES * a
                    if r == 0:
                        win = gp_ref[b0:b0 + sb, lo:lo + rb, c0:c0 + cb]
                    else:
                        win = sh_ref[r - 1, b0:b0 + sb, lo:lo + rb, :]
                    acc = acc + win * wdw_ref[k:k + 1, c0:c0 + cb].reshape(1, 1, cb)
                c_ref[b0:b0 + sb, r0:r0 + rb, c0:c0 + cb] = acc

    conv = c_ref[...].reshape(rows, c)
    act = _silu(_layernorm_rows(conv, lng_ref[...], lnb_ref[...])).astype(BF16)
    m = _dot(act, wout_ref[...]) + bout_ref[...]
    o_ref[...] = (x + m).reshape(bb, tt, d)

    new_hist = gp_ref[:, tt + hp - hist:tt + hp, :]
    gp_ref[:, hp - hist:hp, :] = new_hist

    @pl.when(t == pl.num_programs(1) - 1)
    def _():
        sto_ref[...] = new_hist


def _conv_mixer(x, state, g, w_in, b_in, w_dw, b_dw, ln_g, ln_b, w_out, b_out):
    batch, seq, d = x.shape
    kw, c = w_dw.shape
    hist = kw - 1
    hp = _round_up(hist, SUBLANES)
    bb, tt = _plan_bt(batch, seq, short_rows=ROWS_PER_STEP // 4)
    cb = 2 * LANES
    rb = min(tt, LANES)
    sb = max(1, min(bb, LANES // rb))
    kern = functools.partial(_conv_mixer_kernel, bb=bb, tt=tt, kw=kw, hp=hp, sb=sb, rb=rb, cb=cb)
    x_spec = pl.BlockSpec((bb, tt, d), lambda b, t: (b, t, 0))
    st_spec = pl.BlockSpec((bb, hist, c), lambda b, t: (b, 0, 0))
    return pl.pallas_call(
        kern,
        grid=(batch // bb, seq // tt),
        in_specs=[x_spec, st_spec, _const_spec((1, d)), _const_spec((d, 2 * c)), _const_spec((1, 2 * c)),
                  _const_spec((kw, c)), _const_spec((1, c)), _const_spec((1, c)), _const_spec((1, c)),
                  _const_spec((c, d)), _const_spec((1, d))],
        out_specs=[x_spec, st_spec],
        out_shape=[jax.ShapeDtypeStruct((batch, seq, d), F32), jax.ShapeDtypeStruct((batch, hist, c), F32)],
        scratch_shapes=[pltpu.VMEM((bb, hp + tt, c), F32), pltpu.VMEM((bb, tt, c), F32),
                        pltpu.VMEM((SUBLANES - 1, bb, hp + tt - SUBLANES, cb), F32)],
        compiler_params=_params(2),
        name="conv_mixer",
    )(x, state, g.reshape(1, d), w_in.astype(BF16), b_in.reshape(1, 2 * c), w_dw, b_dw.reshape(1, c),
      ln_g.reshape(1, c), ln_b.reshape(1, c), w_out.astype(BF16), b_out.reshape(1, d))


def _ffn_kernel(*refs, bb, tt, proj, final, chunks, hp):
    refs = list(refs)
    x_ref = refs.pop(0)
    if proj:
        m_ref = refs.pop(0)
        wp_ref = refs.pop(0)
    st_ref, g_ref, wg_ref, wu_ref, wdw_ref, bdw_ref, wd_ref = refs[:7]
    refs = refs[7:]
    if final:
        gf_ref = refs.pop(0)
    o_ref, sto_ref, gpad_ref, carry_ref = refs
    hist = wdw_ref.shape[0] - 1
    t = pl.program_id(1)
    d = x_ref.shape[-1]
    rows = bb * tt

    @pl.when(t == 0)
    def _():
        carry_ref[...] = st_ref[...]

    x = x_ref[...].reshape(rows, d)
    if proj:
        x = x + _dot(m_ref[...].reshape(rows, m_ref.shape[-1]).astype(BF16), wp_ref[...])
    h = _rms_rows(x, g_ref[...]).astype(BF16)
    acc = x
    base = hp - hist
    for c0, fc in chunks:
        gate = _dot(h, wg_ref[:, c0:c0 + fc]).reshape(bb, tt, fc)
        up = _dot(h, wu_ref[:, c0:c0 + fc])
        gpad_ref[:, base:hp, :fc] = carry_ref[:, :, c0:c0 + fc]
        gpad_ref[:, hp:hp + tt, :fc] = gate
        gc = gate * wdw_ref[hist:hist + 1, c0:c0 + fc].reshape(1, 1, fc) + bdw_ref[:, c0:c0 + fc].reshape(1, 1, fc)
        for k in range(hist):
            gc = gc + gpad_ref[:, base + k:base + k + tt, :fc] * wdw_ref[k:k + 1, c0:c0 + fc].reshape(1, 1, fc)
        carry_ref[:, :, c0:c0 + fc] = gpad_ref[:, tt + base:tt + hp, :fc]
        act = (_silu(gc).reshape(rows, fc) * up).astype(BF16)
        acc = acc + _dot(act, wd_ref[c0:c0 + fc, :])
    if final:
        acc = _rms_rows(acc, gf_ref[...])
    o_ref[...] = acc.reshape(bb, tt, d)
    sto_ref[...] = carry_ref[...]


def _ffn(x, state, g, w_gate, w_up, w_dw, b_dw, w_down, layer, proj=None, final_g=None):
    batch, seq, d = x.shape
    kw, f = w_dw.shape

    def layer_spec(shape):
        return pl.BlockSpec((None,) + shape, lambda *_: (layer, 0, 0), pipeline_mode=pl.Buffered(1))

    hist = kw - 1
    hp = _round_up(hist, SUBLANES)
    bb, tt = _plan_bt(batch, seq, short_rows=ROWS_PER_STEP // 2)
    n_tiles = f // MXU_DIM
    assert f % MXU_DIM == 0
    half = (n_tiles + 1) // 2 * MXU_DIM
    chunks = ((0, half), (half, f - half)) if f > half else ((0, f),)
    kern = functools.partial(_ffn_kernel, bb=bb, tt=tt, proj=proj is not None, final=final_g is not None,
                             chunks=chunks, hp=hp)
    x_spec = pl.BlockSpec((bb, tt, d), lambda b, t: (b, t, 0))
    st_spec = pl.BlockSpec((bb, hist, f), lambda b, t: (b, 0, 0))
    args, specs = [x], [x_spec]
    if proj is not None:
        m, w_proj = proj
        dm = m.shape[-1]
        args += [m, w_proj.astype(BF16)]
        specs += [pl.BlockSpec((bb, tt, dm), lambda b, t: (b, t, 0)), _const_spec((dm, d))]
    args += [state, g.reshape(1, d), w_gate, w_up, w_dw, b_dw.reshape(1, f), w_down]
    specs += [st_spec, _const_spec((1, d)), layer_spec((d, f)), layer_spec((d, f)), _const_spec((kw, f)),
              _const_spec((1, f)), layer_spec((f, d))]
    if final_g is not None:
        args.append(final_g.reshape(1, d))
        specs.append(_const_spec((1, d)))
    return pl.pallas_call(
        kern,
        grid=(batch // bb, seq // tt),
        in_specs=specs,
        out_specs=[x_spec, st_spec],
        out_shape=[jax.ShapeDtypeStruct((batch, seq, d), F32), jax.ShapeDtypeStruct((batch, hist, f), F32)],
        scratch_shapes=[pltpu.VMEM((bb, hp + tt, chunks[0][1]), F32), pltpu.VMEM((bb, hist, f), F32)],
        compiler_params=_params(2),
        name="conv_ffn",
    )(*args)


def _qkv_kernel(x_ref, g_ref, w_ref, q_ref, k_ref, v_ref, *maybe_heads, scale):
    dq = k_ref.shape[-1]
    h = _rms_rows(x_ref[...], g_ref[...]).astype(BF16)
    qkv = _dot(h, w_ref[...])
    q = qkv[:, :dq] * scale
    k = qkv[:, dq:2 * dq]
    v = qkv[:, 2 * dq:]
    k_ref[...] = k
    v_ref[...] = v
    if maybe_heads:
        kh_ref, vt_ref = maybe_heads
        q_ref[...] = q.T.astype(BF16)
        vt_ref[...] = v.T.astype(BF16)
        e = kh_ref.shape[-1]
        for hd in range(kh_ref.shape[0]):
            kh_ref[hd] = k[:, hd * e:(hd + 1) * e].astype(BF16)
    else:
        q_ref[...] = q


def _qkv(x2d, g, w_qkv, scale, seq=None):
    n, d = x2d.shape
    dq = w_qkv.shape[1] // 3
    tm = min(n, ROWS_PER_STEP)
    row_spec = pl.BlockSpec((tm, dq), lambda i: (i, 0))
    rows_f32 = jax.ShapeDtypeStruct((n, dq), F32)
    if seq is None:
        out_shape = [rows_f32] * 3
        out_specs = [row_spec] * 3
    else:
        assert seq % tm == 0
        per_seq = seq // tm
        e = dq // DA_HEADS
        t_shape = jax.ShapeDtypeStruct((n // seq, dq, seq), BF16)
        t_spec = pl.BlockSpec((None, dq, tm), lambda i: (i // per_seq, 0, i % per_seq))
        out_shape = [t_shape, rows_f32, rows_f32, jax.ShapeDtypeStruct((n // seq, DA_HEADS, seq, e), BF16), t_shape]
        out_specs = [t_spec, row_spec, row_spec,
                     pl.BlockSpec((None, DA_HEADS, tm, e), lambda i: (i // per_seq, 0, i % per_seq, 0)), t_spec]
    return pl.pallas_call(
        functools.partial(_qkv_kernel, scale=scale),
        grid=(n // tm,),
        in_specs=[pl.BlockSpec((tm, d), lambda i: (i, 0)), _const_spec((1, d)), _const_spec((d, 3 * dq))],
        out_specs=out_specs,
        out_shape=out_shape,
        compiler_params=_params(1),
        name="qkv",
    )(x2d, g.reshape(1, d), w_qkv.astype(BF16))


def _subln(o, ng, lam_init):
    return o * lax.rsqrt(jnp.mean(o * o, axis=-1, keepdims=True) + NORM_EPS) * ng * (1.0 - lam_init)


def _attn_prompt_kernel(lq1_ref, lk1_ref, lq2_ref, lk2_ref, ng_ref, qt_ref, k_ref, vt_ref, o_ref, sa_ref, sb_ref,
                        *, tq, nt, lam_init):
    j = pl.program_id(2)
    e = qt_ref.shape[0]
    w = 2 * tq
    qt = qt_ref[...].astype(F32)
    row = lax.broadcasted_iota(jnp.int32, qt.shape, 0)
    lo = jnp.where(row < e // 2, qt, 0.0)
    hi = jnp.where(row >= e // 2, qt, 0.0)
    qq = jnp.concatenate([half[:, g * tq:(g + 1) * tq] for g in range(nt) for half in (lo, hi)],
                         axis=1).astype(BF16)

    def scores(c, dst_ref, first_col):
        start = pl.multiple_of(c * tq, tq)
        dst_ref[:, first_col:] = _dot(k_ref[pl.ds(start, tq), :], qq[:, first_col:])

    def absorb(src_ref, c, carry, tile, own_chunk):
        m, l, acc = carry
        start = pl.multiple_of(c * tq, tq)
        s = src_ref[:, tile * w:(tile + 1) * w]
        if own_chunk:
            key = lax.broadcasted_iota(jnp.int32, s.shape, 0)
            col = lax.broadcasted_iota(jnp.int32, s.shape, 1)
            s = jnp.where(key <= jnp.where(col >= tq, col - tq, col), s, -jnp.inf)
        m_new = jnp.maximum(m, jnp.max(s, axis=0, keepdims=True))
        p = jnp.exp2(s - m_new)
        alpha = jnp.exp2(m - m_new)
        l = alpha * l + jnp.sum(p, axis=0, keepdims=True)
        return m_new, l, alpha * acc + _dot(vt_ref[:, pl.ds(start, tq)], p.astype(BF16))

    init = (jnp.full((1, w), -jnp.inf, F32), jnp.zeros((1, w), F32), jnp.zeros((e, w), F32))
    scores(0, sa_ref, 0)

    def two_chunks(i, carries):
        scores(2 * i + 1, sb_ref, 0)
        carries = tuple(absorb(sa_ref, 2 * i, c, g, False) for g, c in enumerate(carries))
        scores(2 * i + 2, sa_ref, 0)
        return tuple(absorb(sb_ref, 2 * i + 1, c, g, False) for g, c in enumerate(carries))

    assert nt % 2 == 0
    first = nt * j
    carries = list(lax.fori_loop(0, (nt // 2) * j, two_chunks, (init,) * nt))
    bufs = (sa_ref, sb_ref)
    for d in range(nt):
        if d + 1 < nt:
            scores(first + d + 1, bufs[(d + 1) % 2], (d + 1) * w)
        for g in range(d, nt):
            carries[g] = absorb(bufs[d % 2], first + d, carries[g], g, g == d)

    lam = _diff_lambda(lq1_ref, lk1_ref, lq2_ref, lk2_ref, lam_init)
    for tile, (_, l, acc) in enumerate(carries):
        o = acc * (1.0 / l)
        o = (o[:, :tq] - lam * o[:, tq:]).T
        o_ref[tile * tq:(tile + 1) * tq, :] = _subln(o, ng_ref[...], lam_init).astype(o_ref.dtype)


def _attn_prompt(qt, k, vt, lqk, norm_g, lam_init, tq, nt):
    batch, dq, seq = qt.shape
    e = dq // DA_HEADS
    dl = lqk[0].shape[0]
    assert seq % (nt * tq) == 0
    qt_spec = pl.BlockSpec((None, e, nt * tq), lambda b, h, i: (b, h, i))
    k_spec = pl.BlockSpec((None, None, seq, e), lambda b, h, i: (b, h, 0, 0))
    vt_spec = pl.BlockSpec((None, e, seq), lambda b, h, i: (b, h, 0))
    return pl.pallas_call(
        functools.partial(_attn_prompt_kernel, tq=tq, nt=nt, lam_init=lam_init),
        grid=(batch, DA_HEADS, seq // (nt * tq)),
        in_specs=[_const_spec((1, dl))] * 4 + [pl.BlockSpec((1, e), lambda b, h, i: (0, h)), qt_spec, k_spec, vt_spec],
        out_specs=pl.BlockSpec((None, nt * tq, e), lambda b, h, i: (b, i, h)),
        out_shape=jax.ShapeDtypeStruct((batch, seq, dq), BF16),
        scratch_shapes=[pltpu.VMEM((tq, 2 * nt * tq), F32), pltpu.VMEM((tq, 2 * nt * tq), F32)],
        compiler_params=_params(3),
        name="attn_prompt",
    )(*[a.reshape(1, dl) for a in lqk], norm_g.reshape(1, dq), qt, k, vt)


def _attn_sample_kernel(pt_ref, lq1_ref, lk1_ref, lq2_ref, lk2_ref, ng_ref, q_ref, kn_ref, vn_ref, *rest,
                        n_pages, lam_init):
    del pt_ref
    k_pages = rest[:n_pages]
    v_pages = rest[n_pages:2 * n_pages]
    o_ref, s_ref = rest[2 * n_pages:]
    t, nh, e = q_ref.shape
    page = k_pages[0].shape[0]
    rows = t * nh
    cols = page * nh
    nt = (((1,), (1,)), ((), ()))

    q2 = q_ref[...].reshape(rows, e)
    lane = lax.broadcasted_iota(jnp.int32, q2.shape, 1)
    wq = jnp.concatenate([jnp.where(lane < e // 2, q2, 0.0), jnp.where(lane >= e // 2, q2, 0.0)], axis=0).astype(BF16)

    r = lax.broadcasted_iota(jnp.int32, (2 * rows, cols), 0)
    c = lax.broadcasted_iota(jnp.int32, (2 * rows, cols), 1)
    same_head = _imod(c - r, nh) == 0

    rn = lax.broadcasted_iota(jnp.int32, (2 * rows, rows), 0)
    cn = lax.broadcasted_iota(jnp.int32, (2 * rows, rows), 1)
    keep_new = (_imod(cn - rn, nh) == 0) & (_idiv(cn, nh) <= _idiv(_imod(rn, rows), nh))
    s_new = lax.dot_general(wq, kn_ref[...].reshape(rows, e).astype(BF16), nt, preferred_element_type=F32)
    s_new = jnp.where(keep_new, s_new, -jnp.inf)

    mx = None
    for p in range(n_pages):
        k2 = k_pages[p][...].reshape(cols, e).astype(BF16)
        s = lax.dot_general(wq, k2, nt, preferred_element_type=F32)
        s_ref[:, p * cols:(p + 1) * cols] = s
        mx = s if mx is None else jnp.maximum(mx, s)
    mx = jnp.where(same_head, mx, -jnp.inf)
    m = jnp.maximum(jnp.max(mx, axis=-1, keepdims=True), jnp.max(s_new, axis=-1, keepdims=True))

    shift = jnp.where(same_head, -m, -jnp.inf)
    p_new = jnp.exp2(s_new - m)
    acc = _dot(p_new.astype(BF16), vn_ref[...].reshape(rows, e).astype(BF16))
    tot = None
    for p in range(n_pages):
        pe = jnp.exp2(s_ref[:, p * cols:(p + 1) * cols] + shift)
        tot = pe if tot is None else tot + pe
        acc = acc + _dot(pe.astype(BF16), v_pages[p][...].reshape(cols, e).astype(BF16))
    l = jnp.sum(tot, axis=-1, keepdims=True) + jnp.sum(p_new, axis=-1, keepdims=True)

    o = acc * (1.0 / l)
    lam = _diff_lambda(lq1_ref, lk1_ref, lq2_ref, lk2_ref, lam_init)
    o = o[:rows] - lam * o[rows:]
    ng = jnp.concatenate([ng_ref[...]] * t, axis=0)
    o_ref[...] = _subln(o, ng, lam_init).reshape(t, nh, e)


def _attn_sample(q, k_new, v_new, cache_k, cache_v, page_table, lqk, norm_g, lam_init):
    batch, t, nh, e = q.shape
    page = cache_k.shape[1]
    n_pages = page_table.shape[1]
    dl = lqk[0].shape[0]
    row_spec = pl.BlockSpec((None, t, nh, e), lambda b, pt: (b, 0, 0, 0))

    def page_spec(p):
        return pl.BlockSpec((None, page, nh, e), lambda b, pt: (pt[b, p], 0, 0, 0))

    page_specs = [page_spec(p) for p in range(n_pages)]
    grid_spec = pltpu.PrefetchScalarGridSpec(
        num_scalar_prefetch=1,
        grid=(batch,),
        in_specs=[_const_spec((1, dl))] * 4 + [_const_spec((nh, e)), row_spec, row_spec, row_spec]
        + page_specs + page_specs,
        out_specs=row_spec,
        scratch_shapes=[pltpu.VMEM((2 * t * nh, n_pages * page * nh), F32)],
    )
    return pl.pallas_call(
        functools.partial(_attn_sample_kernel, n_pages=n_pages, lam_init=lam_init),
        grid_spec=grid_spec,
        out_shape=jax.ShapeDtypeStruct((batch, t, nh, e), F32),
        compiler_params=_params(1),
        name="attn_sample",
    )(page_table, *[a.reshape(1, dl) for a in lqk], norm_g.reshape(nh, e), q, k_new, v_new,
      *([cache_k] * n_pages), *([cache_v] * n_pages))


def _pool_kernel(x_ref, st_ref, g_ref, w_ref, sc_ref, o_ref, sto_ref, hp_ref, lv_ref, *, bb, tt, windows, pos0,
                 hist, hp):
    t = pl.program_id(1)
    d = x_ref.shape[-1]
    cg = d // len(windows)
    rows = bb * tt
    lead = SUBLANES

    @pl.when(t == 0)
    def _():
        hp_ref[:, :hp - hist, :] = jnp.zeros((bb, hp - hist, d), F32)
        hp_ref[:, hp - hist:hp, :] = st_ref[...]
        lv_ref[:, :, :lead, :] = jnp.zeros((2, bb, lead, cg), F32)

    x = x_ref[...].reshape(rows, d)
    h = _rms_rows(x, g_ref[...]).reshape(bb, tt, d)
    hp_ref[:, hp:hp + tt, :] = h
    pos = pos0 + t * tt + lax.broadcasted_iota(jnp.int32, (1, tt, 1), 1)
    ys = []
    for gi, win in enumerate(windows):
        assert win & (win - 1) == 0 and win <= hp - lead
        c0 = gi * cg
        cur = h[:, :, c0:c0 + cg]

        def level_rows(level, lo, hi):
            if level == 0:
                return hp_ref[:, lo:hi, c0:c0 + cg]
            return lv_ref[level % 2, :, lo:hi, :]

        n_levels = win.bit_length() - 1
        for level in range(n_levels - 1):
            w = 1 << level
            lv_ref[(level + 1) % 2, :, lead:hp + tt, :] = (level_rows(level, lead, hp + tt)
                                                            + level_rows(level, lead - w, hp + tt - w))
        w = win // 2
        tot = level_rows(n_levels - 1, hp, hp + tt) + level_rows(n_levels - 1, hp - w, hp + tt - w)
        cnt = jnp.minimum(win, pos + 1).astype(F32)
        pooled = (tot / cnt - cur).reshape(rows, cg).astype(BF16)
        ys.append(_dot(pooled, w_ref[gi]))
    y = jnp.concatenate(ys, axis=-1) * sc_ref[...]
    o_ref[...] = (x + y).reshape(bb, tt, d)

    new_hist = hp_ref[:, tt + hp - hist:tt + hp, :]
    hp_ref[:, hp - hist:hp, :] = new_hist

    @pl.when(t == pl.num_programs(1) - 1)
    def _():
        sto_ref[...] = new_hist


def _pool_mixer(x, state, g, w_grp, scale, pos0):
    batch, seq, d = x.shape
    hist = max(POOL_WINDOWS) - 1
    hp = _round_up(hist, SUBLANES) + SUBLANES
    n_grp, cg, _ = w_grp.shape
    bb, tt = _plan_bt(batch, seq, short_rows=ROWS_PER_STEP // 2)
    x_spec = pl.BlockSpec((bb, tt, d), lambda b, t: (b, t, 0))
    st_spec = pl.BlockSpec((bb, hist, d), lambda b, t: (b, 0, 0))
    return pl.pallas_call(
        functools.partial(_pool_kernel, bb=bb, tt=tt, windows=POOL_WINDOWS, pos0=pos0, hist=hist, hp=hp),
        grid=(batch // bb, seq // tt),
        in_specs=[x_spec, st_spec, _const_spec((1, d)), _const_spec((n_grp, cg, cg)), _const_spec((1, d))],
        out_specs=[x_spec, st_spec],
        out_shape=[jax.ShapeDtypeStruct((batch, seq, d), F32), jax.ShapeDtypeStruct((batch, hist, d), F32)],
        scratch_shapes=[pltpu.VMEM((bb, hp + tt, d), F32), pltpu.VMEM((2, bb, hp + tt, cg), F32)],
        compiler_params=_params(2),
        name="pool_mixer",
    )(x, state, g.reshape(1, d), w_grp.astype(BF16), scale.reshape(1, d))


def _sg_kernel(x_ref, g_ref, win_ref, bin_ref, lng_ref, lnb_ref, ws_ref, bs_ref, wout_ref, o_ref, *rest,
               tm, sub, chunk):
    *maybe_v, z_ref = rest
    sg = wout_ref.shape[0]
    cg = sg // SG_GROUPS
    r = lax.broadcasted_iota(jnp.int32, (SG_CHUNK, SG_CHUNK), 0)
    c = lax.broadcasted_iota(jnp.int32, (SG_CHUNK, SG_CHUNK), 1)
    keep = (_idiv(r, chunk) == _idiv(c, chunk)) & (c <= r)
    wss = [jnp.where(keep, ws_ref[gi], 0.0).astype(BF16) for gi in range(SG_GROUPS)]

    def project(i):
        h = _rms_rows(x_ref[i * sub:(i + 1) * sub, :], g_ref[...]).astype(BF16)
        z_ref[i % 2] = _dot(h, win_ref[...])

    def finish(i):
        rows = slice(i * sub, (i + 1) * sub)
        z = _gelu_tanh(z_ref[i % 2] + bin_ref[...])
        u = z[:, :sg]
        v = _layernorm_rows(z[:, sg:], lng_ref[...], lnb_ref[...])
        if maybe_v:
            maybe_v[0][rows, :] = v
        vb = v.astype(BF16)
        acc = x_ref[rows, :]
        for gi in range(SG_GROUPS):
            parts = []
            for r0 in range(0, sub, SG_CHUNK):
                parts.append(_dot(wss[gi], vb[r0:r0 + SG_CHUNK, gi * cg:(gi + 1) * cg]) + bs_ref[gi])
            s = jnp.concatenate(parts, axis=0) if len(parts) > 1 else parts[0]
            gated = (u[:, gi * cg:(gi + 1) * cg] * s).astype(BF16)
            acc = acc + _dot(gated, wout_ref[gi * cg:(gi + 1) * cg, :])
        o_ref[rows, :] = acc

    n_sub = tm // sub
    project(0)
    for i in range(n_sub):
        if i + 1 < n_sub:
            project(i + 1)
        finish(i)


def _sg_mixer(x2d, g, w_in, b_in, ln_g, ln_b, w_s, b_s, w_out, chunk, emit_v, tm):
    n, d = x2d.shape
    sg = w_out.shape[0]
    reps = SG_CHUNK // chunk
    ws = jnp.tile(w_s[:, :chunk, :chunk], (1, reps, reps))
    bs = jnp.tile(b_s[:, :chunk], (1, reps)).reshape(SG_GROUPS, SG_CHUNK, 1)
    row_spec = pl.BlockSpec((tm, d), lambda i: (i, 0))
    out_shape = [jax.ShapeDtypeStruct((n, d), F32)]
    out_specs = [row_spec]
    if emit_v:
        out_shape.append(jax.ShapeDtypeStruct((n, sg), F32))
        out_specs.append(pl.BlockSpec((tm, sg), lambda i: (i, 0)))
    return pl.pallas_call(
        functools.partial(_sg_kernel, tm=tm, sub=min(tm, SG_SUB_ROWS), chunk=chunk),
        grid=(n // tm,),
        in_specs=[row_spec, _const_spec((1, d)), _const_spec((d, 2 * sg)), _const_spec((1, 2 * sg)),
                  _const_spec((1, sg)), _const_spec((1, sg)), _const_spec((SG_GROUPS, SG_CHUNK, SG_CHUNK)),
                  _const_spec((SG_GROUPS, SG_CHUNK, 1)), _const_spec((sg, d))],
        out_specs=out_specs,
        out_shape=out_shape,
        scratch_shapes=[pltpu.VMEM((2, min(tm, SG_SUB_ROWS), 2 * sg), F32)],
        compiler_params=_params(1),
        name="sg_mixer",
    )(x2d, g.reshape(1, d), w_in.astype(BF16), b_in.reshape(1, 2 * sg), ln_g.reshape(1, sg), ln_b.reshape(1, sg),
      ws, bs, w_out.astype(BF16))


def kernel(x_prompt, x_sample, state_conv, cache_k, cache_v, page_table, state_pool, state_ffn, norm_mix, norm_ffn, norm_final, cv_w_in, cv_b_in, cv_w_dw, cv_b_dw, cv_ln_g, cv_ln_b, cv_w_out, cv_b_out, da_w_qkv, da_lq1, da_lk1, da_lq2, da_lk2, da_norm_g, da_w_o, pl_w, pl_scale, sg_w_in, sg_b_in, sg_ln_g, sg_ln_b, sg_w_s, sg_b_s, sg_w_out, ff_w_gate, ff_w_up, ff_w_dw, ff_b_dw, ff_w_down):
    bp, seq, d = x_prompt.shape
    bs, dec_seq, _ = x_sample.shape
    depth, ffn_kw, d_ff = ff_w_dw.shape
    past_len = page_table.shape[1] * cache_k.shape[1]
    head_dim = d // (2 * DA_HEADS)
    kv_shape = (DA_HEADS, 2 * head_dim)

    wg_all, wu_all, wd_all = ff_w_gate.astype(BF16), ff_w_up.astype(BF16), ff_w_down.astype(BF16)

    def ffn(i, x, state, **kw):
        return _ffn(x, state, norm_ffn[i], wg_all, wu_all, ff_w_dw[i], ff_b_dw[i], wd_all, i, **kw)

    ffn_zero = jnp.zeros((bp, ffn_kw - 1, d_ff), F32)
    ffn_p, ffn_s = [], []

    cw = (norm_mix[0], cv_w_in, cv_b_in, cv_w_dw, cv_b_dw, cv_ln_g, cv_ln_b, cv_w_out, cv_b_out)
    xp, conv_p = _conv_mixer(x_prompt, jnp.zeros((bp,) + state_conv.shape[1:], F32), *cw)
    xs, conv_s = _conv_mixer(x_sample, state_conv, *cw)
    xp, st = ffn(0, xp, ffn_zero)
    ffn_p.append(st)
    xs, st = ffn(0, xs, state_ffn[0])
    ffn_s.append(st)

    lam_init = 0.8 - 0.6 * math.exp(-0.3 * 1)
    lqk = (da_lq1, da_lk1, da_lq2, da_lk2)
    scale = head_dim ** -0.5 * math.log2(math.e)
    qt_p, k_rows_p, v_rows_p, kh_p, vt_p = _qkv(xp.reshape(bp * seq, d), norm_mix[1], da_w_qkv, scale, seq=seq)
    o_p = _attn_prompt(qt_p, kh_p, vt_p, lqk, da_norm_g, lam_init, tq=min(seq // ATTN_TILES_PER_STEP, ATTN_TILE),
                       nt=ATTN_TILES_PER_STEP)
    xp, st = ffn(1, xp, ffn_zero, proj=(o_p, da_w_o))
    ffn_p.append(st)
    q_s, k_rows_s, v_rows_s = _qkv(xs.reshape(bs * dec_seq, d), norm_mix[1], da_w_qkv, scale)
    rows_s = (bs, dec_seq) + kv_shape
    o_s = _attn_sample(q_s.reshape(rows_s), k_rows_s.reshape(rows_s), v_rows_s.reshape(rows_s),
                       cache_k, cache_v, page_table, lqk, da_norm_g, lam_init)
    xs, st = ffn(1, xs, state_ffn[1], proj=(o_s.reshape(bs, dec_seq, d), da_w_o))
    ffn_s.append(st)

    xp, pool_p = _pool_mixer(xp, jnp.zeros((bp,) + state_pool.shape[1:], F32), norm_mix[2], pl_w, pl_scale, 0)
    xs, pool_s = _pool_mixer(xs, state_pool, norm_mix[2], pl_w, pl_scale, past_len)
    xp, st = ffn(2, xp, ffn_zero)
    ffn_p.append(st)
    xs, st = ffn(2, xs, state_ffn[2])
    ffn_s.append(st)

    sw = (norm_mix[3], sg_w_in, sg_b_in, sg_ln_g, sg_ln_b, sg_w_s, sg_b_s, sg_w_out)
    assert seq % SG_CHUNK == 0 and past_len % SG_CHUNK == 0 and SG_CHUNK % dec_seq == 0
    (xp2,) = _sg_mixer(xp.reshape(bp * seq, d), *sw, chunk=SG_CHUNK, emit_v=False, tm=min(seq, 2 * SG_SUB_ROWS))
    xs2, sg_v = _sg_mixer(xs.reshape(bs * dec_seq, d), *sw, chunk=dec_seq, emit_v=True, tm=2 * LANES)
    y_prompt, st = ffn(3, xp2.reshape(bp, seq, d), ffn_zero, final_g=norm_final)
    ffn_p.append(st)
    y_sample, st = ffn(3, xs2.reshape(bs, dec_seq, d), state_ffn[3], final_g=norm_final)
    ffn_s.append(st)

    return (y_prompt, y_sample, conv_p, conv_s,
            k_rows_p.reshape((bp, seq) + kv_shape), v_rows_p.reshape((bp, seq) + kv_shape),
            k_rows_s.reshape((bs, dec_seq) + kv_shape), v_rows_s.reshape((bs, dec_seq) + kv_shape),
            pool_p, pool_s, sg_v.reshape(bs, dec_seq, -1), jnp.stack(ffn_p, axis=0), jnp.stack(ffn_s, axis=0))
```

```python
import functools
import math

import jax
import jax.numpy as jnp
from jax import lax
from jax.experimental import pallas as pl
from jax.experimental.pallas import tpu as pltpu

F32 = jnp.float32
BF16 = jnp.bfloat16
NORM_EPS = 1e-6
POOL_WINDOWS = (2, 4, 8, 16)
SG_CHUNK = 128
SG_GROUPS = 4
DA_HEADS = 8

SUBLANES = 8
LANES = 128
MXU_DIM = 256
VMEM_LIMIT_BYTES = 56 * 1024 * 1024
ROWS_PER_STEP = 512
SG_SUB_ROWS = 256
ATTN_TILE = 512
ATTN_TILES_PER_STEP = 8


def _round_up(n, m):
    return -(-n // m) * m


def _plan_bt(batch, seq, short_rows):
    if seq >= ROWS_PER_STEP:
        assert seq % ROWS_PER_STEP == 0
        return 1, ROWS_PER_STEP
    bb = max(1, min(batch, short_rows // seq))
    assert batch % bb == 0 and seq % SUBLANES == 0
    return bb, seq


def _const_spec(shape):
    nd = len(shape)
    return pl.BlockSpec(shape, lambda *_: (0,) * nd, pipeline_mode=pl.Buffered(1))


def _params(n_grid):
    return pltpu.CompilerParams(dimension_semantics=("arbitrary",) * n_grid,
                                vmem_limit_bytes=VMEM_LIMIT_BYTES)


def _rms_rows(x, g):
    return x * lax.rsqrt(jnp.mean(x * x, axis=-1, keepdims=True) + NORM_EPS) * g


def _layernorm_rows(x, g, b):
    mu = jnp.mean(x, axis=-1, keepdims=True)
    xc = x - mu
    return xc * lax.rsqrt(jnp.mean(xc * xc, axis=-1, keepdims=True) + NORM_EPS) * g + b


def _sigmoid(x):
    return 1.0 / (1.0 + jnp.exp(-x))


def _silu(x):
    return x * _sigmoid(x)


def _gelu_tanh(x):
    return x * (0.5 * (1.0 + jnp.tanh(math.sqrt(2.0 / math.pi) * (x + 0.044715 * (x * x * x)))))


def _idiv(x, n):
    assert n & (n - 1) == 0
    return x >> (n.bit_length() - 1)


def _imod(x, n):
    assert n & (n - 1) == 0
    return x & (n - 1)


def _dot(a, b):
    return jnp.dot(a, b, preferred_element_type=F32)


def _diff_lambda(lq1_ref, lk1_ref, lq2_ref, lk2_ref, lam_init):
    a = jnp.sum(lq1_ref[...] * lk1_ref[...], axis=-1, keepdims=True)
    b = jnp.sum(lq2_ref[...] * lk2_ref[...], axis=-1, keepdims=True)
    return jnp.exp(a) - jnp.exp(b) + lam_init


def _conv_mixer_kernel(x_ref, st_ref, g_ref, win_ref, bin_ref, wdw_ref, bdw_ref, lng_ref, lnb_ref,
                       wout_ref, bout_ref, o_ref, sto_ref, gp_ref, c_ref, sh_ref, *, bb, tt, kw, hp, sb, rb, cb):
    hist = kw - 1
    t = pl.program_id(1)
    d = x_ref.shape[-1]
    c = gp_ref.shape[-1]
    rows = bb * tt

    @pl.when(t == 0)
    def _():
        gp_ref[:, hp - hist:hp, :] = st_ref[...]

    x = x_ref[...].reshape(rows, d)
    h = _rms_rows(x, g_ref[...]).astype(BF16)
    ag = _dot(h, win_ref[...]) + bin_ref[...]
    glu = ag[:, :c] * _sigmoid(ag[:, c:])
    gp_ref[:, hp:hp + tt, :] = glu.reshape(bb, tt, c)

    base = hp - hist
    span = sh_ref.shape[2]
    for c0 in range(0, c, cb):
        for r in range(1, SUBLANES):
            sh_ref[r - 1] = gp_ref[:, r:r + span, c0:c0 + cb]
        for b0 in range(0, bb, sb):
            for r0 in range(0, tt, rb):
                acc = jnp.broadcast_to(bdw_ref[:, c0:c0 + cb].reshape(1, 1, cb), (sb, rb, cb))
                for k in range(kw):
                    a, r = divmod(base + k, SUBLANES)
                    lo = r0 + SUBLANES * a
                    if r == 0:
                        win = gp_ref[b0:b0 + sb, lo:lo + rb, c0:c0 + cb]
                    else:
                        win = sh_ref[r - 1, b0:b0 + sb, lo:lo + rb, :]
                    acc = acc + win * wdw_ref[k:k + 1, c0:c0 + cb].reshape(1, 1, cb)
                c_ref[b0:b0 + sb, r0:r0 + rb, c0:c0 + cb] = acc

    conv = c_ref[...].reshape(rows, c)
    act = _silu(_layernorm_rows(conv, lng_ref[...], lnb_ref[...])).astype(BF16)
    m = _dot(act, wout_ref[...]) + bout_ref[...]
    o_ref[...] = (x + m).reshape(bb, tt, d)

    new_hist = gp_ref[:, tt + hp - hist:tt + hp, :]
    gp_ref[:, hp - hist:hp, :] = new_hist

    @pl.when(t == pl.num_programs(1) - 1)
    def _():
        sto_ref[...] = new_hist


def _conv_mixer(x, state, g, w_in, b_in, w_dw, b_dw, ln_g, ln_b, w_out, b_out):
    batch, seq, d = x.shape
    kw, c = w_dw.shape
    hist = kw - 1
    hp = _round_up(hist, SUBLANES)
    bb, tt = _plan_bt(batch, seq, short_rows=ROWS_PER_STEP // 4)
    cb = 2 * LANES
    rb = min(tt, LANES)
    sb = max(1, min(bb, LANES // rb))
    kern = functools.partial(_conv_mixer_kernel, bb=bb, tt=tt, kw=kw, hp=hp, sb=sb, rb=rb, cb=cb)
    x_spec = pl.BlockSpec((bb, tt, d), lambda b, t: (b, t, 0))
    st_spec = pl.BlockSpec((bb, hist, c), lambda b, t: (b, 0, 0))
    return pl.pallas_call(
        kern,
        grid=(batch // bb, seq // tt),
        in_specs=[x_spec, st_spec, _const_spec((1, d)), _const_spec((d, 2 * c)), _const_spec((1, 2 * c)),
                  _const_spec((kw, c)), _const_spec((1, c)), _const_spec((1, c)), _const_spec((1, c)),
                  _const_spec((c, d)), _const_spec((1, d))],
        out_specs=[x_spec, st_spec],
        out_shape=[jax.ShapeDtypeStruct((batch, seq, d), F32), jax.ShapeDtypeStruct((batch, hist, c), F32)],
        scratch_shapes=[pltpu.VMEM((bb, hp + tt, c), F32), pltpu.VMEM((bb, tt, c), F32),
                        pltpu.VMEM((SUBLANES - 1, bb, hp + tt - SUBLANES, cb), F32)],
        compiler_params=_params(2),
        name="conv_mixer",
    )(x, state, g.reshape(1, d), w_in.astype(BF16), b_in.reshape(1, 2 * c), w_dw, b_dw.reshape(1, c),
      ln_g.reshape(1, c), ln_b.reshape(1, c), w_out.astype(BF16), b_out.reshape(1, d))


def _ffn_kernel(*refs, bb, tt, proj, final, chunks, hp):
    refs = list(refs)
    x_ref = refs.pop(0)
    if proj:
        m_ref = refs.pop(0)
        wp_ref = refs.pop(0)
    st_ref, g_ref, wg_ref, wu_ref, wdw_ref, bdw_ref, wd_ref = refs[:7]
    refs = refs[7:]
    if final:
        gf_ref = refs.pop(0)
    o_ref, sto_ref, gpad_ref, carry_ref = refs
    hist = wdw_ref.shape[0] - 1
    t = pl.program_id(1)
    d = x_ref.shape[-1]
    rows = bb * tt

    @pl.when(t == 0)
    def _():
        carry_ref[...] = st_ref[...]

    x = x_ref[...].reshape(rows, d)
    if proj:
        x = x + _dot(m_ref[...].reshape(rows, m_ref.shape[-1]).astype(BF16), wp_ref[...])
    h = _rms_rows(x, g_ref[...]).astype(BF16)
    acc = x
    base = hp - hist
    for c0, fc in chunks:
        gate = _dot(h, wg_ref[:, c0:c0 + fc]).reshape(bb, tt, fc)
        up = _dot(h, wu_ref[:, c0:c0 + fc])
        gpad_ref[:, base:hp, :fc] = carry_ref[:, :, c0:c0 + fc]
        gpad_ref[:, hp:hp + tt, :fc] = gate
        gc = gate * wdw_ref[hist:hist + 1, c0:c0 + fc].reshape(1, 1, fc) + bdw_ref[:, c0:c0 + fc].reshape(1, 1, fc)
        for k in range(hist):
            gc = gc + gpad_ref[:, base + k:base + k + tt, :fc] * wdw_ref[k:k + 1, c0:c0 + fc].reshape(1, 1, fc)
        carry_ref[:, :, c0:c0 + fc] = gpad_ref[:, tt + base:tt + hp, :fc]
        act = (_silu(gc).reshape(rows, fc) * up).astype(BF16)
        acc = acc + _dot(act, wd_ref[c0:c0 + fc, :])
    if final:
        acc = _rms_rows(acc, gf_ref[...])
    o_ref[...] = acc.reshape(bb, tt, d)
    sto_ref[...] = carry_ref[...]


def _ffn(x, state, g, w_gate, w_up, w_dw, b_dw, w_down, layer, proj=None, final_g=None):
    batch, seq, d = x.shape
    kw, f = w_dw.shape

    def layer_spec(shape):
        return pl.BlockSpec((None,) + shape, lambda *_: (layer, 0, 0), pipeline_mode=pl.Buffered(1))

    hist = kw - 1
    hp = _round_up(hist, SUBLANES)
    bb, tt = _plan_bt(batch, seq, short_rows=ROWS_PER_STEP // 2)
    assert f % MXU_DIM == 0
    chunks = ((0, f),)
    kern = functools.partial(_ffn_kernel, bb=bb, tt=tt, proj=proj is not None, final=final_g is not None,
                             chunks=chunks, hp=hp)
    x_spec = pl.BlockSpec((bb, tt, d), lambda b, t: (b, t, 0))
    st_spec = pl.BlockSpec((bb, hist, f), lambda b, t: (b, 0, 0))
    args, specs = [x], [x_spec]
    if proj is not None:
        m, w_proj = proj
        dm = m.shape[-1]
        args += [m, w_proj.astype(BF16)]
        specs += [pl.BlockSpec((bb, tt, dm), lambda b, t: (b, t, 0)), _const_spec((dm, d))]
    args += [state, g.reshape(1, d), w_gate, w_up, w_dw, b_dw.reshape(1, f), w_down]
    specs += [st_spec, _const_spec((1, d)), layer_spec((d, f)), layer_spec((d, f)), _const_spec((kw, f)),
              _const_spec((1, f)), layer_spec((f, d))]
    if final_g is not None:
        args.append(final_g.reshape(1, d))
        specs.append(_const_spec((1, d)))
    return pl.pallas_call(
        kern,
        grid=(batch // bb, seq // tt),
        in_specs=specs,
        out_specs=[x_spec, st_spec],
        out_shape=[jax.ShapeDtypeStruct((batch, seq, d), F32), jax.ShapeDtypeStruct((batch, hist, f), F32)],
        scratch_shapes=[pltpu.VMEM((bb, hp + tt, chunks[0][1]), F32), pltpu.VMEM((bb, hist, f), F32)],
        compiler_params=_params(2),
        name="conv_ffn",
    )(*args)


def _qkv_kernel(x_ref, g_ref, w_ref, q_ref, k_ref, v_ref, *maybe_heads, scale):
    dq = k_ref.shape[-1]
    h = _rms_rows(x_ref[...], g_ref[...]).astype(BF16)
    qkv = _dot(h, w_ref[...])
    q = qkv[:, :dq] * scale
    k = qkv[:, dq:2 * dq]
    v = qkv[:, 2 * dq:]
    k_ref[...] = k
    v_ref[...] = v
    if maybe_heads:
        kh_ref, vt_ref = maybe_heads
        q_ref[...] = q.T.astype(BF16)
        vt_ref[...] = v.T.astype(BF16)
        e = kh_ref.shape[-1]
        for hd in range(kh_ref.shape[0]):
            kh_ref[hd] = k[:, hd * e:(hd + 1) * e].astype(BF16)
    else:
        q_ref[...] = q


def _qkv(x2d, g, w_qkv, scale, seq=None):
    n, d = x2d.shape
    dq = w_qkv.shape[1] // 3
    tm = min(n, ROWS_PER_STEP)
    row_spec = pl.BlockSpec((tm, dq), lambda i: (i, 0))
    rows_f32 = jax.ShapeDtypeStruct((n, dq), F32)
    if seq is None:
        out_shape = [rows_f32] * 3
        out_specs = [row_spec] * 3
    else:
        assert seq % tm == 0
        per_seq = seq // tm
        e = dq // DA_HEADS
        t_shape = jax.ShapeDtypeStruct((n // seq, dq, seq), BF16)
        t_spec = pl.BlockSpec((None, dq, tm), lambda i: (i // per_seq, 0, i % per_seq))
        out_shape = [t_shape, rows_f32, rows_f32, jax.ShapeDtypeStruct((n // seq, DA_HEADS, seq, e), BF16), t_shape]
        out_specs = [t_spec, row_spec, row_spec,
                     pl.BlockSpec((None, DA_HEADS, tm, e), lambda i: (i // per_seq, 0, i % per_seq, 0)), t_spec]
    return pl.pallas_call(
        functools.partial(_qkv_kernel, scale=scale),
        grid=(n // tm,),
        in_specs=[pl.BlockSpec((tm, d), lambda i: (i, 0)), _const_spec((1, d)), _const_spec((d, 3 * dq))],
        out_specs=out_specs,
        out_shape=out_shape,
        compiler_params=_params(1),
        name="qkv",
    )(x2d, g.reshape(1, d), w_qkv.astype(BF16))


def _subln(o, ng, lam_init):
    return o * lax.rsqrt(jnp.mean(o * o, axis=-1, keepdims=True) + NORM_EPS) * ng * (1.0 - lam_init)


def _attn_prompt_kernel(lq1_ref, lk1_ref, lq2_ref, lk2_ref, ng_ref, qt_ref, k_ref, vt_ref, o_ref, sa_ref, sb_ref,
                        *, tq, nt, lam_init):
    j = pl.program_id(2)
    e = qt_ref.shape[0]
    w = 2 * tq
    qt = qt_ref[...].astype(F32)
    row = lax.broadcasted_iota(jnp.int32, qt.shape, 0)
    lo = jnp.where(row < e // 2, qt, 0.0)
    hi = jnp.where(row >= e // 2, qt, 0.0)
    qq = jnp.concatenate([half[:, g * tq:(g + 1) * tq] for g in range(nt) for half in (lo, hi)],
                         axis=1).astype(BF16)

    def scores(c, dst_ref, first_col):
        start = pl.multiple_of(c * tq, tq)
        dst_ref[:, first_col:] = _dot(k_ref[pl.ds(start, tq), :], qq[:, first_col:])

    def absorb(src_ref, c, carry, tile, own_chunk):
        m, l, acc = carry
        start = pl.multiple_of(c * tq, tq)
        s = src_ref[:, tile * w:(tile + 1) * w]
        if own_chunk:
            key = lax.broadcasted_iota(jnp.int32, s.shape, 0)
            col = lax.broadcasted_iota(jnp.int32, s.shape, 1)
            s = jnp.where(key <= jnp.where(col >= tq, col - tq, col), s, -jnp.inf)
        m_new = jnp.maximum(m, jnp.max(s, axis=0, keepdims=True))
        p = jnp.exp2(s - m_new)
        alpha = jnp.exp2(m - m_new)
        l = alpha * l + jnp.sum(p, axis=0, keepdims=True)
        return m_new, l, alpha * acc + _dot(vt_ref[:, pl.ds(start, tq)], p.astype(BF16))

    init = (jnp.full((1, w), -jnp.inf, F32), jnp.zeros((1, w), F32), jnp.zeros((e, w), F32))
    scores(0, sa_ref, 0)

    def two_chunks(i, carries):
        scores(2 * i + 1, sb_ref, 0)
        carries = tuple(absorb(sa_ref, 2 * i, c, g, False) for g, c in enumerate(carries))
        scores(2 * i + 2, sa_ref, 0)
        return tuple(absorb(sb_ref, 2 * i + 1, c, g, False) for g, c in enumerate(carries))

    assert nt % 2 == 0
    first = nt * j
    carries = list(lax.fori_loop(0, (nt // 2) * j, two_chunks, (init,) * nt))
    bufs = (sa_ref, sb_ref)
    for d in range(nt):
        if d + 1 < nt:
            scores(first + d + 1, bufs[(d + 1) % 2], (d + 1) * w)
        for g in range(d, nt):
            carries[g] = absorb(bufs[d % 2], first + d, carries[g], g, g == d)

    lam = _diff_lambda(lq1_ref, lk1_ref, lq2_ref, lk2_ref, lam_init)
    for tile, (_, l, acc) in enumerate(carries):
        o = acc * (1.0 / l)
        o = (o[:, :tq] - lam * o[:, tq:]).T
        o_ref[tile * tq:(tile + 1) * tq, :] = _subln(o, ng_ref[...], lam_init).astype(o_ref.dtype)


def _attn_prompt(qt, k, vt, lqk, norm_g, lam_init, tq, nt):
    batch, dq, seq = qt.shape
    e = dq // DA_HEADS
    dl = lqk[0].shape[0]
    assert seq % (nt * tq) == 0
    qt_spec = pl.BlockSpec((None, e, nt * tq), lambda b, h, i: (b, h, i))
    k_spec = pl.BlockSpec((None, None, seq, e), lambda b, h, i: (b, h, 0, 0))
    vt_spec = pl.BlockSpec((None, e, seq), lambda b, h, i: (b, h, 0))
    return pl.pallas_call(
        functools.partial(_attn_prompt_kernel, tq=tq, nt=nt, lam_init=lam_init),
        grid=(batch, DA_HEADS, seq // (nt * tq)),
        in_specs=[_const_spec((1, dl))] * 4 + [pl.BlockSpec((1, e), lambda b, h, i: (0, h)), qt_spec, k_spec, vt_spec],
        out_specs=pl.BlockSpec((None, nt * tq, e), lambda b, h, i: (b, i, h)),
        out_shape=jax.ShapeDtypeStruct((batch, seq, dq), BF16),
        scratch_shapes=[pltpu.VMEM((tq, 2 * nt * tq), F32), pltpu.VMEM((tq, 2 * nt * tq), F32)],
        compiler_params=_params(3),
        name="attn_prompt",
    )(*[a.reshape(1, dl) for a in lqk], norm_g.reshape(1, dq), qt, k, vt)


def _attn_sample_kernel(pt_ref, lq1_ref, lk1_ref, lq2_ref, lk2_ref, ng_ref, q_ref, kn_ref, vn_ref, *rest,
                        n_pages, lam_init):
    del pt_ref
    k_pages = rest[:n_pages]
    v_pages = rest[n_pages:2 * n_pages]
    o_ref, s_ref = rest[2 * n_pages:]
    t, nh, e = q_ref.shape
    page = k_pages[0].shape[0]
    rows = t * nh
    cols = page * nh
    nt = (((1,), (1,)), ((), ()))

    q2 = q_ref[...].reshape(rows, e)
    lane = lax.broadcasted_iota(jnp.int32, q2.shape, 1)
    wq = jnp.concatenate([jnp.where(lane < e // 2, q2, 0.0), jnp.where(lane >= e // 2, q2, 0.0)], axis=0).astype(BF16)

    r = lax.broadcasted_iota(jnp.int32, (2 * rows, cols), 0)
    c = lax.broadcasted_iota(jnp.int32, (2 * rows, cols), 1)
    same_head = _imod(c - r, nh) == 0

    rn = lax.broadcasted_iota(jnp.int32, (2 * rows, rows), 0)
    cn = lax.broadcasted_iota(jnp.int32, (2 * rows, rows), 1)
    keep_new = (_imod(cn - rn, nh) == 0) & (_idiv(cn, nh) <= _idiv(_imod(rn, rows), nh))
    s_new = lax.dot_general(wq, kn_ref[...].reshape(rows, e).astype(BF16), nt, preferred_element_type=F32)
    s_new = jnp.where(keep_new, s_new, -jnp.inf)

    mx = None
    for p in range(n_pages):
        k2 = k_pages[p][...].reshape(cols, e).astype(BF16)
        s = lax.dot_general(wq, k2, nt, preferred_element_type=F32)
        s_ref[:, p * cols:(p + 1) * cols] = s
        mx = s if mx is None else jnp.maximum(mx, s)
    mx = jnp.where(same_head, mx, -jnp.inf)
    m = jnp.maximum(jnp.max(mx, axis=-1, keepdims=True), jnp.max(s_new, axis=-1, keepdims=True))

    shift = jnp.where(same_head, -m, -jnp.inf)
    p_new = jnp.exp2(s_new - m)
    acc = _dot(p_new.astype(BF16), vn_ref[...].reshape(rows, e).astype(BF16))
    tot = None
    for p in range(n_pages):
        pe = jnp.exp2(s_ref[:, p * cols:(p + 1) * cols] + shift)
        tot = pe if tot is None else tot + pe
        acc = acc + _dot(pe.astype(BF16), v_pages[p][...].reshape(cols, e).astype(BF16))
    l = jnp.sum(tot, axis=-1, keepdims=True) + jnp.sum(p_new, axis=-1, keepdims=True)

    o = acc * (1.0 / l)
    lam = _diff_lambda(lq1_ref, lk1_ref, lq2_ref, lk2_ref, lam_init)
    o = o[:rows] - lam * o[rows:]
    ng = jnp.concatenate([ng_ref[...]] * t, axis=0)
    o_ref[...] = _subln(o, ng, lam_init).reshape(t, nh, e)


def _attn_sample(q, k_new, v_new, cache_k, cache_v, page_table, lqk, norm_g, lam_init):
    batch, t, nh, e = q.shape
    page = cache_k.shape[1]
    n_pages = page_table.shape[1]
    dl = lqk[0].shape[0]
    row_spec = pl.BlockSpec((None, t, nh, e), lambda b, pt: (b, 0, 0, 0))

    def page_spec(p):
        return pl.BlockSpec((None, page, nh, e), lambda b, pt: (pt[b, p], 0, 0, 0))

    page_specs = [page_spec(p) for p in range(n_pages)]
    grid_spec = pltpu.PrefetchScalarGridSpec(
        num_scalar_prefetch=1,
        grid=(batch,),
        in_specs=[_const_spec((1, dl))] * 4 + [_const_spec((nh, e)), row_spec, row_spec, row_spec]
        + page_specs + page_specs,
        out_specs=row_spec,
        scratch_shapes=[pltpu.VMEM((2 * t * nh, n_pages * page * nh), F32)],
    )
    return pl.pallas_call(
        functools.partial(_attn_sample_kernel, n_pages=n_pages, lam_init=lam_init),
        grid_spec=grid_spec,
        out_shape=jax.ShapeDtypeStruct((batch, t, nh, e), F32),
        compiler_params=_params(1),
        name="attn_sample",
    )(page_table, *[a.reshape(1, dl) for a in lqk], norm_g.reshape(nh, e), q, k_new, v_new,
      *([cache_k] * n_pages), *([cache_v] * n_pages))


def _pool_kernel(x_ref, st_ref, g_ref, w_ref, sc_ref, o_ref, sto_ref, hp_ref, lv_ref, *, bb, tt, windows, pos0,
                 hist, hp):
    t = pl.program_id(1)
    d = x_ref.shape[-1]
    cg = d // len(windows)
    rows = bb * tt
    lead = SUBLANES

    @pl.when(t == 0)
    def _():
        hp_ref[:, :hp - hist, :] = jnp.zeros((bb, hp - hist, d), F32)
        hp_ref[:, hp - hist:hp, :] = st_ref[...]
        lv_ref[:, :, :lead, :] = jnp.zeros((2, bb, lead, cg), F32)

    x = x_ref[...].reshape(rows, d)
    h = _rms_rows(x, g_ref[...]).reshape(bb, tt, d)
    hp_ref[:, hp:hp + tt, :] = h
    pos = pos0 + t * tt + lax.broadcasted_iota(jnp.int32, (1, tt, 1), 1)
    ys = []
    for gi, win in enumerate(windows):
        assert win & (win - 1) == 0 and win <= hp - lead
        c0 = gi * cg
        cur = h[:, :, c0:c0 + cg]

        def level_rows(level, lo, hi):
            if level == 0:
                return hp_ref[:, lo:hi, c0:c0 + cg]
            return lv_ref[level % 2, :, lo:hi, :]

        n_levels = win.bit_length() - 1
        for level in range(n_levels - 1):
            w = 1 << level
            lv_ref[(level + 1) % 2, :, lead:hp + tt, :] = (level_rows(level, lead, hp + tt)
                                                            + level_rows(level, lead - w, hp + tt - w))
        w = win // 2
        tot = level_rows(n_levels - 1, hp, hp + tt) + level_rows(n_levels - 1, hp - w, hp + tt - w)
        cnt = jnp.minimum(win, pos + 1).astype(F32)
        pooled = (tot / cnt - cur).reshape(rows, cg).astype(BF16)
        ys.append(_dot(pooled, w_ref[gi]))
    y = jnp.concatenate(ys, axis=-1) * sc_ref[...]
    o_ref[...] = (x + y).reshape(bb, tt, d)

    new_hist = hp_ref[:, tt + hp - hist:tt + hp, :]
    hp_ref[:, hp - hist:hp, :] = new_hist

    @pl.when(t == pl.num_programs(1) - 1)
    def _():
        sto_ref[...] = new_hist


def _pool_mixer(x, state, g, w_grp, scale, pos0):
    batch, seq, d = x.shape
    hist = max(POOL_WINDOWS) - 1
    hp = _round_up(hist, SUBLANES) + SUBLANES
    n_grp, cg, _ = w_grp.shape
    bb, tt = _plan_bt(batch, seq, short_rows=ROWS_PER_STEP // 2)
    x_spec = pl.BlockSpec((bb, tt, d), lambda b, t: (b, t, 0))
    st_spec = pl.BlockSpec((bb, hist, d), lambda b, t: (b, 0, 0))
    return pl.pallas_call(
        functools.partial(_pool_kernel, bb=bb, tt=tt, windows=POOL_WINDOWS, pos0=pos0, hist=hist, hp=hp),
        grid=(batch // bb, seq // tt),
        in_specs=[x_spec, st_spec, _const_spec((1, d)), _const_spec((n_grp, cg, cg)), _const_spec((1, d))],
        out_specs=[x_spec, st_spec],
        out_shape=[jax.ShapeDtypeStruct((batch, seq, d), F32), jax.ShapeDtypeStruct((batch, hist, d), F32)],
        scratch_shapes=[pltpu.VMEM((bb, hp + tt, d), F32), pltpu.VMEM((2, bb, hp + tt, cg), F32)],
        compiler_params=_params(2),
        name="pool_mixer",
    )(x, state, g.reshape(1, d), w_grp.astype(BF16), scale.reshape(1, d))


def _sg_kernel(x_ref, g_ref, win_ref, bin_ref, lng_ref, lnb_ref, ws_ref, bs_ref, wout_ref, o_ref, *rest,
               tm, sub, chunk):
    *maybe_v, z_ref = rest
    sg = wout_ref.shape[0]
    cg = sg // SG_GROUPS
    r = lax.broadcasted_iota(jnp.int32, (SG_CHUNK, SG_CHUNK), 0)
    c = lax.broadcasted_iota(jnp.int32, (SG_CHUNK, SG_CHUNK), 1)
    keep = (_idiv(r, chunk) == _idiv(c, chunk)) & (c <= r)
    wss = [jnp.where(keep, ws_ref[gi], 0.0).astype(BF16) for gi in range(SG_GROUPS)]

    def project(i):
        h = _rms_rows(x_ref[i * sub:(i + 1) * sub, :], g_ref[...]).astype(BF16)
        z_ref[i % 2] = _dot(h, win_ref[...])

    def finish(i):
        rows = slice(i * sub, (i + 1) * sub)
        z = _gelu_tanh(z_ref[i % 2] + bin_ref[...])
        u = z[:, :sg]
        v = _layernorm_rows(z[:, sg:], lng_ref[...], lnb_ref[...])
        if maybe_v:
            maybe_v[0][rows, :] = v
        vb = v.astype(BF16)
        gated = []
        for gi in range(SG_GROUPS):
            parts = []
            for r0 in range(0, sub, SG_CHUNK):
                parts.append(_dot(wss[gi], vb[r0:r0 + SG_CHUNK, gi * cg:(gi + 1) * cg]) + bs_ref[gi])
            s = jnp.concatenate(parts, axis=0) if len(parts) > 1 else parts[0]
            gated.append((u[:, gi * cg:(gi + 1) * cg] * s).astype(BF16))
        o_ref[rows, :] = x_ref[rows, :] + _dot(jnp.concatenate(gated, axis=-1), wout_ref[...])

    n_sub = tm // sub
    project(0)
    for i in range(n_sub):
        if i + 1 < n_sub:
            project(i + 1)
        finish(i)


def _sg_mixer(x2d, g, w_in, b_in, ln_g, ln_b, w_s, b_s, w_out, chunk, emit_v, tm):
    n, d = x2d.shape
    sg = w_out.shape[0]
    reps = SG_CHUNK // chunk
    ws = jnp.tile(w_s[:, :chunk, :chunk], (1, reps, reps))
    bs = jnp.tile(b_s[:, :chunk], (1, reps)).reshape(SG_GROUPS, SG_CHUNK, 1)
    row_spec = pl.BlockSpec((tm, d), lambda i: (i, 0))
    out_shape = [jax.ShapeDtypeStruct((n, d), F32)]
    out_specs = [row_spec]
    if emit_v:
        out_shape.append(jax.ShapeDtypeStruct((n, sg), F32))
        out_specs.append(pl.BlockSpec((tm, sg), lambda i: (i, 0)))
    return pl.pallas_call(
        functools.partial(_sg_kernel, tm=tm, sub=min(tm, SG_SUB_ROWS), chunk=chunk),
        grid=(n // tm,),
        in_specs=[row_spec, _const_spec((1, d)), _const_spec((d, 2 * sg)), _const_spec((1, 2 * sg)),
                  _const_spec((1, sg)), _const_spec((1, sg)), _const_spec((SG_GROUPS, SG_CHUNK, SG_CHUNK)),
                  _const_spec((SG_GROUPS, SG_CHUNK, 1)), _const_spec((sg, d))],
        out_specs=out_specs,
        out_shape=out_shape,
        scratch_shapes=[pltpu.VMEM((2, min(tm, SG_SUB_ROWS), 2 * sg), F32)],
        compiler_params=_params(1),
        name="sg_mixer",
    )(x2d, g.reshape(1, d), w_in.astype(BF16), b_in.reshape(1, 2 * sg), ln_g.reshape(1, sg), ln_b.reshape(1, sg),
      ws, bs, w_out.astype(BF16))


def kernel(x_prompt, x_sample, state_conv, cache_k, cache_v, page_table, state_pool, state_ffn, norm_mix, norm_ffn, norm_final, cv_w_in, cv_b_in, cv_w_dw, cv_b_dw, cv_ln_g, cv_ln_b, cv_w_out, cv_b_out, da_w_qkv, da_lq1, da_lk1, da_lq2, da_lk2, da_norm_g, da_w_o, pl_w, pl_scale, sg_w_in, sg_b_in, sg_ln_g, sg_ln_b, sg_w_s, sg_b_s, sg_w_out, ff_w_gate, ff_w_up, ff_w_dw, ff_b_dw, ff_w_down):
    bp, seq, d = x_prompt.shape
    bs, dec_seq, _ = x_sample.shape
    depth, ffn_kw, d_ff = ff_w_dw.shape
    past_len = page_table.shape[1] * cache_k.shape[1]
    head_dim = d // (2 * DA_HEADS)
    kv_shape = (DA_HEADS, 2 * head_dim)

    wg_all, wu_all, wd_all = ff_w_gate.astype(BF16), ff_w_up.astype(BF16), ff_w_down.astype(BF16)

    def ffn(i, x, state, **kw):
        return _ffn(x, state, norm_ffn[i], wg_all, wu_all, ff_w_dw[i], ff_b_dw[i], wd_all, i, **kw)

    ffn_zero = jnp.zeros((bp, ffn_kw - 1, d_ff), F32)
    ffn_p, ffn_s = [], []

    cw = (norm_mix[0], cv_w_in, cv_b_in, cv_w_dw, cv_b_dw, cv_ln_g, cv_ln_b, cv_w_out, cv_b_out)
    xp, conv_p = _conv_mixer(x_prompt, jnp.zeros((bp,) + state_conv.shape[1:], F32), *cw)
    xs, conv_s = _conv_mixer(x_sample, state_conv, *cw)
    xp, st = ffn(0, xp, ffn_zero)
    ffn_p.append(st)
    xs, st = ffn(0, xs, state_ffn[0])
    ffn_s.append(st)

    lam_init = 0.8 - 0.6 * math.exp(-0.3 * 1)
    lqk = (da_lq1, da_lk1, da_lq2, da_lk2)
    scale = head_dim ** -0.5 * math.log2(math.e)
    qt_p, k_rows_p, v_rows_p, kh_p, vt_p = _qkv(xp.reshape(bp * seq, d), norm_mix[1], da_w_qkv, scale, seq=seq)
    o_p = _attn_prompt(qt_p, kh_p, vt_p, lqk, da_norm_g, lam_init, tq=min(seq // ATTN_TILES_PER_STEP, ATTN_TILE),
                       nt=ATTN_TILES_PER_STEP)
    xp, st = ffn(1, xp, ffn_zero, proj=(o_p, da_w_o))
    ffn_p.append(st)
    q_s, k_rows_s, v_rows_s = _qkv(xs.reshape(bs * dec_seq, d), norm_mix[1], da_w_qkv, scale)
    rows_s = (bs, dec_seq) + kv_shape
    o_s = _attn_sample(q_s.reshape(rows_s), k_rows_s.reshape(rows_s), v_rows_s.reshape(rows_s),
                       cache_k, cache_v, page_table, lqk, da_norm_g, lam_init)
    xs, st = ffn(1, xs, state_ffn[1], proj=(o_s.reshape(bs, dec_seq, d), da_w_o))
    ffn_s.append(st)

    xp, pool_p = _pool_mixer(xp, jnp.zeros((bp,) + state_pool.shape[1:], F32), norm_mix[2], pl_w, pl_scale, 0)
    xs, pool_s = _pool_mixer(xs, state_pool, norm_mix[2], pl_w, pl_scale, past_len)
    xp, st = ffn(2, xp, ffn_zero)
    ffn_p.append(st)
    xs, st = ffn(2, xs, state_ffn[2])
    ffn_s.append(st)

    sw = (norm_mix[3], sg_w_in, sg_b_in, sg_ln_g, sg_ln_b, sg_w_s, sg_b_s, sg_w_out)
    assert seq % SG_CHUNK == 0 and past_len % SG_CHUNK == 0 and SG_CHUNK % dec_seq == 0
    (xp2,) = _sg_mixer(xp.reshape(bp * seq, d), *sw, chunk=SG_CHUNK, emit_v=False, tm=min(seq, 2 * SG_SUB_ROWS))
    xs2, sg_v = _sg_mixer(xs.reshape(bs * dec_seq, d), *sw, chunk=dec_seq, emit_v=True, tm=2 * LANES)
    y_prompt, st = ffn(3, xp2.reshape(bp, seq, d), ffn_zero, final_g=norm_final)
    ffn_p.append(st)
    y_sample, st = ffn(3, xs2.reshape(bs, dec_seq, d), state_ffn[3], final_g=norm_final)
    ffn_s.append(st)

    return (y_prompt, y_sample, conv_p, conv_s,
            k_rows_p.reshape((bp, seq) + kv_shape), v_rows_p.reshape((bp, seq) + kv_shape),
            k_rows_s.reshape((bs, dec_seq) + kv_shape), v_rows_s.reshape((bs, dec_seq) + kv_shape),
            pool_p, pool_s, sg_v.reshape(bs, dec_seq, -1), jnp.stack(ffn_p, axis=0), jnp.stack(ffn_s, axis=0))
```

```python
import functools
import math

import jax
import jax.numpy as jnp
from jax import lax
from jax.experimental import pallas as pl
from jax.experimental.pallas import tpu as pltpu

F32 = jnp.float32
BF16 = jnp.bfloat16
NORM_EPS = 1e-6
POOL_WINDOWS = (2, 4, 8, 16)
SG_CHUNK = 128
SG_GROUPS = 4
DA_HEADS = 8

SUBLANES = 8
LANES = 128
MXU_DIM = 256
VMEM_LIMIT_BYTES = 56 * 1024 * 1024
ROWS_PER_STEP = 512
SG_SUB_ROWS = 256
ATTN_TILE = 512
ATTN_TILES_PER_STEP = 8


def _round_up(n, m):
    return -(-n // m) * m


def _plan_bt(batch, seq, short_rows):
    if seq >= ROWS_PER_STEP:
        assert seq % ROWS_PER_STEP == 0
        return 1, ROWS_PER_STEP
    bb = max(1, min(batch, short_rows // seq))
    assert batch % bb == 0 and seq % SUBLANES == 0
    return bb, seq


def _const_spec(shape):
    nd = len(shape)
    return pl.BlockSpec(shape, lambda *_: (0,) * nd, pipeline_mode=pl.Buffered(1))


def _params(n_grid):
    return pltpu.CompilerParams(dimension_semantics=("arbitrary",) * n_grid,
                                vmem_limit_bytes=VMEM_LIMIT_BYTES)


def _rms_rows(x, g):
    return x * lax.rsqrt(jnp.mean(x * x, axis=-1, keepdims=True) + NORM_EPS) * g


def _layernorm_rows(x, g, b):
    mu = jnp.mean(x, axis=-1, keepdims=True)
    xc = x - mu
    return xc * lax.rsqrt(jnp.mean(xc * xc, axis=-1, keepdims=True) + NORM_EPS) * g + b


def _sigmoid(x):
    return 1.0 / (1.0 + jnp.exp(-x))


def _silu(x):
    return x * _sigmoid(x)


def _gelu_tanh(x):
    return x * (0.5 * (1.0 + jnp.tanh(math.sqrt(2.0 / math.pi) * (x + 0.044715 * (x * x * x)))))


def _idiv(x, n):
    assert n & (n - 1) == 0
    return x >> (n.bit_length() - 1)


def _imod(x, n):
    assert n & (n - 1) == 0
    return x & (n - 1)


def _dot(a, b):
    return jnp.dot(a, b, preferred_element_type=F32)


def _diff_lambda(lq1_ref, lk1_ref, lq2_ref, lk2_ref, lam_init):
    a = jnp.sum(lq1_ref[...] * lk1_ref[...], axis=-1, keepdims=True)
    b = jnp.sum(lq2_ref[...] * lk2_ref[...], axis=-1, keepdims=True)
    return jnp.exp(a) - jnp.exp(b) + lam_init


def _conv_mixer_kernel(x_ref, st_ref, g_ref, win_ref, bin_ref, wdw_ref, bdw_ref, lng_ref, lnb_ref,
                       wout_ref, bout_ref, o_ref, sto_ref, gp_ref, c_ref, sh_ref, *, bb, tt, kw, hp, sb, rb, cb):
    hist = kw - 1
    t = pl.program_id(1)
    d = x_ref.shape[-1]
    c = gp_ref.shape[-1]
    rows = bb * tt

    @pl.when(t == 0)
    def _():
        gp_ref[:, hp - hist:hp, :] = st_ref[...]

    x = x_ref[...].reshape(rows, d)
    h = _rms_rows(x, g_ref[...]).astype(BF16)
    ag = _dot(h, win_ref[...]) + bin_ref[...]
    glu = ag[:, :c] * _sigmoid(ag[:, c:])
    gp_ref[:, hp:hp + tt, :] = glu.reshape(bb, tt, c)

    base = hp - hist
    span = sh_ref.shape[2]
    for c0 in range(0, c, cb):
        for r in range(1, SUBLANES):
            sh_ref[r - 1] = gp_ref[:, r:r + span, c0:c0 + cb]
        for b0 in range(0, bb, sb):
            for r0 in range(0, tt, rb):
                acc = jnp.broadcast_to(bdw_ref[:, c0:c0 + cb].reshape(1, 1, cb), (sb, rb, cb))
                for k in range(kw):
                    a, r = divmod(base + k, SUBLANES)
                    lo = r0 + SUBLANES * a
                    if r == 0:
                        win = gp_ref[b0:b0 + sb, lo:lo + rb, c0:c0 + cb]
                    else:
                        win = sh_ref[r - 1, b0:b0 + sb, lo:lo + rb, :]
                    acc = acc + win * wdw_ref[k:k + 1, c0:c0 + cb].reshape(1, 1, cb)
                c_ref[b0:b0 + sb, r0:r0 + rb, c0:c0 + cb] = acc

    conv = c_ref[...].reshape(rows, c)
    act = _silu(_layernorm_rows(conv, lng_ref[...], lnb_ref[...])).astype(BF16)
    m = _dot(act, wout_ref[...]) + bout_ref[...]
    o_ref[...] = (x + m).reshape(bb, tt, d)

    new_hist = gp_ref[:, tt + hp - hist:tt + hp, :]
    gp_ref[:, hp - hist:hp, :] = new_hist

    @pl.when(t == pl.num_programs(1) - 1)
    def _():
        sto_ref[...] = new_hist


def _conv_mixer(x, state, g, w_in, b_in, w_dw, b_dw, ln_g, ln_b, w_out, b_out):
    batch, seq, d = x.shape
    kw, c = w_dw.shape
    hist = kw - 1
    hp = _round_up(hist, SUBLANES)
    bb, tt = _plan_bt(batch, seq, short_rows=ROWS_PER_STEP // 4)
    cb = 2 * LANES
    rb = min(tt, LANES)
    sb = max(1, min(bb, LANES // rb))
    kern = functools.partial(_conv_mixer_kernel, bb=bb, tt=tt, kw=kw, hp=hp, sb=sb, rb=rb, cb=cb)
    x_spec = pl.BlockSpec((bb, tt, d), lambda b, t: (b, t, 0))
    st_spec = pl.BlockSpec((bb, hist, c), lambda b, t: (b, 0, 0))
    return pl.pallas_call(
        kern,
        grid=(batch // bb, seq // tt),
        in_specs=[x_spec, st_spec, _const_spec((1, d)), _const_spec((d, 2 * c)), _const_spec((1, 2 * c)),
                  _const_spec((kw, c)), _const_spec((1, c)), _const_spec((1, c)), _const_spec((1, c)),
                  _const_spec((c, d)), _const_spec((1, d))],
        out_specs=[x_spec, st_spec],
        out_shape=[jax.ShapeDtypeStruct((batch, seq, d), F32), jax.ShapeDtypeStruct((batch, hist, c), F32)],
        scratch_shapes=[pltpu.VMEM((bb, hp + tt, c), F32), pltpu.VMEM((bb, tt, c), F32),
                        pltpu.VMEM((SUBLANES - 1, bb, hp + tt - SUBLANES, cb), F32)],
        compiler_params=_params(2),
        name="conv_mixer",
    )(x, state, g.reshape(1, d), w_in.astype(BF16), b_in.reshape(1, 2 * c), w_dw, b_dw.reshape(1, c),
      ln_g.reshape(1, c), ln_b.reshape(1, c), w_out.astype(BF16), b_out.reshape(1, d))


def _ffn_kernel(*refs, bb, tt, proj, final, chunks, hp):
    refs = list(refs)
    x_ref = refs.pop(0)
    if proj:
        m_ref = refs.pop(0)
        wp_ref = refs.pop(0)
    st_ref, g_ref, wg_ref, wu_ref, wdw_ref, bdw_ref, wd_ref = refs[:7]
    refs = refs[7:]
    if final:
        gf_ref = refs.pop(0)
    o_ref, sto_ref, gpad_ref, carry_ref = refs
    hist = wdw_ref.shape[0] - 1
    t = pl.program_id(1)
    d = x_ref.shape[-1]
    rows = bb * tt

    @pl.when(t == 0)
    def _():
        carry_ref[...] = st_ref[...]

    x = x_ref[...].reshape(rows, d)
    if proj:
        x = x + _dot(m_ref[...].reshape(rows, m_ref.shape[-1]).astype(BF16), wp_ref[...])
    h = _rms_rows(x, g_ref[...]).astype(BF16)
    acc = x
    base = hp - hist
    for c0, fc in chunks:
        gate = _dot(h, wg_ref[:, c0:c0 + fc]).reshape(bb, tt, fc)
        up = _dot(h, wu_ref[:, c0:c0 + fc])
        gpad_ref[:, base:hp, :fc] = carry_ref[:, :, c0:c0 + fc]
        gpad_ref[:, hp:hp + tt, :fc] = gate
        gc = gate * wdw_ref[hist:hist + 1, c0:c0 + fc].reshape(1, 1, fc) + bdw_ref[:, c0:c0 + fc].reshape(1, 1, fc)
        for k in range(hist):
            gc = gc + gpad_ref[:, base + k:base + k + tt, :fc] * wdw_ref[k:k + 1, c0:c0 + fc].reshape(1, 1, fc)
        carry_ref[:, :, c0:c0 + fc] = gpad_ref[:, tt + base:tt + hp, :fc]
        act = (_silu(gc).reshape(rows, fc) * up).astype(BF16)
        acc = acc + _dot(act, wd_ref[c0:c0 + fc, :])
    if final:
        acc = _rms_rows(acc, gf_ref[...])
    o_ref[...] = acc.reshape(bb, tt, d)
    sto_ref[...] = carry_ref[...]


def _ffn(x, state, g, w_gate, w_up, w_dw, b_dw, w_down, layer, proj=None, final_g=None):
    batch, seq, d = x.shape
    kw, f = w_dw.shape

    def layer_spec(shape):
        return pl.BlockSpec((None,) + shape, lambda *_: (layer, 0, 0), pipeline_mode=pl.Buffered(1))

    hist = kw - 1
    hp = _round_up(hist, SUBLANES)
    bb, tt = _plan_bt(batch, seq, short_rows=ROWS_PER_STEP // 2)
    assert f % MXU_DIM == 0
    chunks = ((0, f),)
    kern = functools.partial(_ffn_kernel, bb=bb, tt=tt, proj=proj is not None, final=final_g is not None,
                             chunks=chunks, hp=hp)
    x_spec = pl.BlockSpec((bb, tt, d), lambda b, t: (b, t, 0))
    st_spec = pl.BlockSpec((bb, hist, f), lambda b, t: (b, 0, 0))
    args, specs = [x], [x_spec]
    if proj is not None:
        m, w_proj = proj
        dm = m.shape[-1]
        args += [m, w_proj.astype(BF16)]
        specs += [pl.BlockSpec((bb, tt, dm), lambda b, t: (b, t, 0)), _const_spec((dm, d))]
    args += [state, g.reshape(1, d), w_gate, w_up, w_dw, b_dw.reshape(1, f), w_down]
    specs += [st_spec, _const_spec((1, d)), layer_spec((d, f)), layer_spec((d, f)), _const_spec((kw, f)),
              _const_spec((1, f)), layer_spec((f, d))]
    if final_g is not None:
        args.append(final_g.reshape(1, d))
        specs.append(_const_spec((1, d)))
    return pl.pallas_call(
        kern,
        grid=(batch // bb, seq // tt),
        in_specs=specs,
        out_specs=[x_spec, st_spec],
        out_shape=[jax.ShapeDtypeStruct((batch, seq, d), F32), jax.ShapeDtypeStruct((batch, hist, f), F32)],
        scratch_shapes=[pltpu.VMEM((bb, hp + tt, chunks[0][1]), F32), pltpu.VMEM((bb, hist, f), F32)],
        compiler_params=_params(2),
        name="conv_ffn",
    )(*args)


def _qkv_kernel(x_ref, g_ref, w_ref, q_ref, k_ref, v_ref, *maybe_heads, scale):
    dq = k_ref.shape[-1]
    h = _rms_rows(x_ref[...], g_ref[...]).astype(BF16)
    qkv = _dot(h, w_ref[...])
    q = qkv[:, :dq] * scale
    k = qkv[:, dq:2 * dq]
    v = qkv[:, 2 * dq:]
    k_ref[...] = k
    v_ref[...] = v
    if maybe_heads:
        kh_ref, vt_ref = maybe_heads
        q_ref[...] = q.T.astype(BF16)
        vt_ref[...] = v.T.astype(BF16)
        e = kh_ref.shape[-1]
        for hd in range(kh_ref.shape[0]):
            kh_ref[hd] = k[:, hd * e:(hd + 1) * e].astype(BF16)
    else:
        q_ref[...] = q


def _qkv(x2d, g, w_qkv, scale, seq=None):
    n, d = x2d.shape
    dq = w_qkv.shape[1] // 3
    tm = min(n, ROWS_PER_STEP)
    row_spec = pl.BlockSpec((tm, dq), lambda i: (i, 0))
    rows_f32 = jax.ShapeDtypeStruct((n, dq), F32)
    if seq is None:
        out_shape = [rows_f32] * 3
        out_specs = [row_spec] * 3
    else:
        assert seq % tm == 0
        per_seq = seq // tm
        e = dq // DA_HEADS
        t_shape = jax.ShapeDtypeStruct((n // seq, dq, seq), BF16)
        t_spec = pl.BlockSpec((None, dq, tm), lambda i: (i // per_seq, 0, i % per_seq))
        out_shape = [t_shape, rows_f32, rows_f32, jax.ShapeDtypeStruct((n // seq, DA_HEADS, seq, e), BF16), t_shape]
        out_specs = [t_spec, row_spec, row_spec,
                     pl.BlockSpec((None, DA_HEADS, tm, e), lambda i: (i // per_seq, 0, i % per_seq, 0)), t_spec]
    return pl.pallas_call(
        functools.partial(_qkv_kernel, scale=scale),
        grid=(n // tm,),
        in_specs=[pl.BlockSpec((tm, d), lambda i: (i, 0)), _const_spec((1, d)), _const_spec((d, 3 * dq))],
        out_specs=out_specs,
        out_shape=out_shape,
        compiler_params=_params(1),
        name="qkv",
    )(x2d, g.reshape(1, d), w_qkv.astype(BF16))


def _subln(o, ng, lam_init):
    return o * lax.rsqrt(jnp.mean(o * o, axis=-1, keepdims=True) + NORM_EPS) * ng * (1.0 - lam_init)


def _attn_prompt_kernel(lq1_ref, lk1_ref, lq2_ref, lk2_ref, ng_ref, qt_ref, k_ref, vt_ref, o_ref, sa_ref, sb_ref,
                        *, tq, nt, lam_init):
    j = pl.program_id(2)
    e = qt_ref.shape[0]
    w = 2 * tq
    qt = qt_ref[...].astype(F32)
    row = lax.broadcasted_iota(jnp.int32, qt.shape, 0)
    lo = jnp.where(row < e // 2, qt, 0.0)
    hi = jnp.where(row >= e // 2, qt, 0.0)
    qq = jnp.concatenate([half[:, g * tq:(g + 1) * tq] for g in range(nt) for half in (lo, hi)],
                         axis=1).astype(BF16)

    def scores(c, dst_ref, first_col):
        start = pl.multiple_of(c * tq, tq)
        dst_ref[:, first_col:] = _dot(k_ref[pl.ds(start, tq), :], qq[:, first_col:])

    def absorb(src_ref, c, carry, tile, own_chunk):
        m, l, acc = carry
        start = pl.multiple_of(c * tq, tq)
        s = src_ref[:, tile * w:(tile + 1) * w]
        if own_chunk:
            key = lax.broadcasted_iota(jnp.int32, s.shape, 0)
            col = lax.broadcasted_iota(jnp.int32, s.shape, 1)
            s = jnp.where(key <= jnp.where(col >= tq, col - tq, col), s, -jnp.inf)
        m_new = jnp.maximum(m, jnp.max(s, axis=0, keepdims=True))
        p = jnp.exp2(s - m_new)
        alpha = jnp.exp2(m - m_new)
        l = alpha * l + jnp.sum(p, axis=0, keepdims=True)
        return m_new, l, alpha * acc + _dot(vt_ref[:, pl.ds(start, tq)], p.astype(BF16))

    init = (jnp.full((1, w), -jnp.inf, F32), jnp.zeros((1, w), F32), jnp.zeros((e, w), F32))
    scores(0, sa_ref, 0)

    def two_chunks(i, carries):
        scores(2 * i + 1, sb_ref, 0)
        carries = tuple(absorb(sa_ref, 2 * i, c, g, False) for g, c in enumerate(carries))
        scores(2 * i + 2, sa_ref, 0)
        return tuple(absorb(sb_ref, 2 * i + 1, c, g, False) for g, c in enumerate(carries))

    assert nt % 2 == 0
    first = nt * j
    carries = list(lax.fori_loop(0, (nt // 2) * j, two_chunks, (init,) * nt))
    bufs = (sa_ref, sb_ref)
    for d in range(nt):
        if d + 1 < nt:
            scores(first + d + 1, bufs[(d + 1) % 2], (d + 1) * w)
        for g in range(d, nt):
            carries[g] = absorb(bufs[d % 2], first + d, carries[g], g, g == d)

    lam = _diff_lambda(lq1_ref, lk1_ref, lq2_ref, lk2_ref, lam_init)
    for tile, (_, l, acc) in enumerate(carries):
        o = acc * (1.0 / l)
        o = (o[:, :tq] - lam * o[:, tq:]).T
        o_ref[tile * tq:(tile + 1) * tq, :] = _subln(o, ng_ref[...], lam_init).astype(o_ref.dtype)


def _attn_prompt(qt, k, vt, lqk, norm_g, lam_init, tq, nt):
    batch, dq, seq = qt.shape
    e = dq // DA_HEADS
    dl = lqk[0].shape[0]
    assert seq % (nt * tq) == 0
    qt_spec = pl.BlockSpec((None, e, nt * tq), lambda b, h, i: (b, h, i))
    k_spec = pl.BlockSpec((None, None, seq, e), lambda b, h, i: (b, h, 0, 0))
    vt_spec = pl.BlockSpec((None, e, seq), lambda b, h, i: (b, h, 0))
    return pl.pallas_call(
        functools.partial(_attn_prompt_kernel, tq=tq, nt=nt, lam_init=lam_init),
        grid=(batch, DA_HEADS, seq // (nt * tq)),
        in_specs=[_const_spec((1, dl))] * 4 + [pl.BlockSpec((1, e), lambda b, h, i: (0, h)), qt_spec, k_spec, vt_spec],
        out_specs=pl.BlockSpec((None, nt * tq, e), lambda b, h, i: (b, i, h)),
        out_shape=jax.ShapeDtypeStruct((batch, seq, dq), BF16),
        scratch_shapes=[pltpu.VMEM((tq, 2 * nt * tq), F32), pltpu.VMEM((tq, 2 * nt * tq), F32)],
        compiler_params=_params(3),
        name="attn_prompt",
    )(*[a.reshape(1, dl) for a in lqk], norm_g.reshape(1, dq), qt, k, vt)


def _attn_sample_kernel(pt_ref, lq1_ref, lk1_ref, lq2_ref, lk2_ref, ng_ref, q_ref, kn_ref, vn_ref, ck_ref, cv_ref,
                        o_ref, s_ref, buf_ref, sem_ref, *, n_pages, lam_init):
    b = pl.program_id(0)
    nb = pl.num_programs(0)
    t, nh, e = q_ref.shape
    page = buf_ref.shape[2]

    def page_copy(src_ref, seq, p, slot):
        return pltpu.make_async_copy(src_ref.at[pt_ref[seq, p]], buf_ref.at[slot, p], sem_ref.at[slot])

    def start_set(src_ref, seq, slot):
        for p in range(n_pages):
            page_copy(src_ref, seq, p, slot).start()

    def wait_set(src_ref, seq, slot):
        for p in range(n_pages):
            page_copy(src_ref, seq, p, slot).wait()

    k_slot = 2 * (b % 2)
    v_slot = k_slot + 1
    next_v_slot = 3 - k_slot

    @pl.when(b == 0)
    def _():
        start_set(ck_ref, 0, 0)
        start_set(cv_ref, 0, 1)

    @pl.when((b == 0) & (nb > 1))
    def _():
        start_set(ck_ref, 1, 2)

    wait_set(ck_ref, b, k_slot)

    @pl.when(b + 1 < nb)
    def _():
        start_set(cv_ref, b + 1, next_v_slot)

    k_pages = [buf_ref.at[k_slot, p] for p in range(n_pages)]
    v_pages = [buf_ref.at[v_slot, p] for p in range(n_pages)]
    rows = t * nh
    cols = page * nh
    nt = (((1,), (1,)), ((), ()))

    q2 = q_ref[...].reshape(rows, e)
    lane = lax.broadcasted_iota(jnp.int32, q2.shape, 1)
    wq = jnp.concatenate([jnp.where(lane < e // 2, q2, 0.0), jnp.where(lane >= e // 2, q2, 0.0)], axis=0).astype(BF16)

    r = lax.broadcasted_iota(jnp.int32, (2 * rows, cols), 0)
    c = lax.broadcasted_iota(jnp.int32, (2 * rows, cols), 1)
    same_head = _imod(c - r, nh) == 0

    rn = lax.broadcasted_iota(jnp.int32, (2 * rows, rows), 0)
    cn = lax.broadcasted_iota(jnp.int32, (2 * rows, rows), 1)
    keep_new = (_imod(cn - rn, nh) == 0) & (_idiv(cn, nh) <= _idiv(_imod(rn, rows), nh))
    s_new = lax.dot_general(wq, kn_ref[...].reshape(rows, e).astype(BF16), nt, preferred_element_type=F32)
    s_new = jnp.where(keep_new, s_new, -jnp.inf)

    mx = None
    for p in range(n_pages):
        k2 = k_pages[p][...].reshape(cols, e).astype(BF16)
        s = lax.dot_general(wq, k2, nt, preferred_element_type=F32)
        s_ref[:, p * cols:(p + 1) * cols] = s
        mx = s if mx is None else jnp.maximum(mx, s)
    mx = jnp.where(same_head, mx, -jnp.inf)
    m = jnp.maximum(jnp.max(mx, axis=-1, keepdims=True), jnp.max(s_new, axis=-1, keepdims=True))

    wait_set(cv_ref, b, v_slot)

    @pl.when(b + 2 < nb)
    def _():
        start_set(ck_ref, b + 2, k_slot)

    shift = jnp.where(same_head, -m, -jnp.inf)
    p_new = jnp.exp2(s_new - m)
    acc = _dot(p_new.astype(BF16), vn_ref[...].reshape(rows, e).astype(BF16))
    tot = None
    for p in range(n_pages):
        pe = jnp.exp2(s_ref[:, p * cols:(p + 1) * cols] + shift)
        tot = pe if tot is None else tot + pe
        acc = acc + _dot(pe.astype(BF16), v_pages[p][...].reshape(cols, e).astype(BF16))
    l = jnp.sum(tot, axis=-1, keepdims=True) + jnp.sum(p_new, axis=-1, keepdims=True)

    o = acc * (1.0 / l)
    lam = _diff_lambda(lq1_ref, lk1_ref, lq2_ref, lk2_ref, lam_init)
    o = o[:rows] - lam * o[rows:]
    ng = jnp.concatenate([ng_ref[...]] * t, axis=0)
    o_ref[...] = _subln(o, ng, lam_init).reshape(t, nh, e)


def _attn_sample(q, k_new, v_new, cache_k, cache_v, page_table, lqk, norm_g, lam_init):
    batch, t, nh, e = q.shape
    page = cache_k.shape[1]
    n_pages = page_table.shape[1]
    dl = lqk[0].shape[0]
    row_spec = pl.BlockSpec((None, t, nh, e), lambda b, pt: (b, 0, 0, 0))

    hbm_spec = pl.BlockSpec(memory_space=pl.ANY)
    grid_spec = pltpu.PrefetchScalarGridSpec(
        num_scalar_prefetch=1,
        grid=(batch,),
        in_specs=[_const_spec((1, dl))] * 4 + [_const_spec((nh, e)), row_spec, row_spec, row_spec, hbm_spec, hbm_spec],
        out_specs=row_spec,
        scratch_shapes=[pltpu.VMEM((2 * t * nh, n_pages * page * nh), F32),
                        pltpu.VMEM((4, n_pages, page, nh, e), F32),
                        pltpu.SemaphoreType.DMA((4,))],
    )
    return pl.pallas_call(
        functools.partial(_attn_sample_kernel, n_pages=n_pages, lam_init=lam_init),
        grid_spec=grid_spec,
        out_shape=jax.ShapeDtypeStruct((batch, t, nh, e), F32),
        compiler_params=_params(1),
        name="attn_sample",
    )(page_table, *[a.reshape(1, dl) for a in lqk], norm_g.reshape(nh, e), q, k_new, v_new, cache_k, cache_v)


def _pool_kernel(x_ref, st_ref, g_ref, w_ref, sc_ref, o_ref, sto_ref, hp_ref, lv_ref, *, bb, tt, windows, pos0,
                 hist, hp):
    t = pl.program_id(1)
    d = x_ref.shape[-1]
    cg = d // len(windows)
    rows = bb * tt
    lead = SUBLANES

    @pl.when(t == 0)
    def _():
        hp_ref[:, :hp - hist, :] = jnp.zeros((bb, hp - hist, d), F32)
        hp_ref[:, hp - hist:hp, :] = st_ref[...]
        lv_ref[:, :, :lead, :] = jnp.zeros((2, bb, lead, cg), F32)

    x = x_ref[...].reshape(rows, d)
    h = _rms_rows(x, g_ref[...]).reshape(bb, tt, d)
    hp_ref[:, hp:hp + tt, :] = h
    pos = pos0 + t * tt + lax.broadcasted_iota(jnp.int32, (1, tt, 1), 1)
    ys = []
    for gi, win in enumerate(windows):
        assert win & (win - 1) == 0 and win <= hp - lead
        c0 = gi * cg
        cur = h[:, :, c0:c0 + cg]

        def level_rows(level, lo, hi):
            if level == 0:
                return hp_ref[:, lo:hi, c0:c0 + cg]
            return lv_ref[level % 2, :, lo:hi, :]

        n_levels = win.bit_length() - 1
        for level in range(n_levels - 1):
            w = 1 << level
            lv_ref[(level + 1) % 2, :, lead:hp + tt, :] = (level_rows(level, lead, hp + tt)
                                                            + level_rows(level, lead - w, hp + tt - w))
        w = win // 2
        tot = level_rows(n_levels - 1, hp, hp + tt) + level_rows(n_levels - 1, hp - w, hp + tt - w)
        cnt = jnp.minimum(win, pos + 1).astype(F32)
        pooled = (tot / cnt - cur).reshape(rows, cg).astype(BF16)
        ys.append(_dot(pooled, w_ref[gi]))
    y = jnp.concatenate(ys, axis=-1) * sc_ref[...]
    o_ref[...] = (x + y).reshape(bb, tt, d)

    new_hist = hp_ref[:, tt + hp - hist:tt + hp, :]
    hp_ref[:, hp - hist:hp, :] = new_hist

    @pl.when(t == pl.num_programs(1) - 1)
    def _():
        sto_ref[...] = new_hist


def _pool_mixer(x, state, g, w_grp, scale, pos0):
    batch, seq, d = x.shape
    hist = max(POOL_WINDOWS) - 1
    hp = _round_up(hist, SUBLANES) + SUBLANES
    n_grp, cg, _ = w_grp.shape
    bb, tt = _plan_bt(batch, seq, short_rows=ROWS_PER_STEP // 2)
    x_spec = pl.BlockSpec((bb, tt, d), lambda b, t: (b, t, 0))
    st_spec = pl.BlockSpec((bb, hist, d), lambda b, t: (b, 0, 0))
    return pl.pallas_call(
        functools.partial(_pool_kernel, bb=bb, tt=tt, windows=POOL_WINDOWS, pos0=pos0, hist=hist, hp=hp),
        grid=(batch // bb, seq // tt),
        in_specs=[x_spec, st_spec, _const_spec((1, d)), _const_spec((n_grp, cg, cg)), _const_spec((1, d))],
        out_specs=[x_spec, st_spec],
        out_shape=[jax.ShapeDtypeStruct((batch, seq, d), F32), jax.ShapeDtypeStruct((batch, hist, d), F32)],
        scratch_shapes=[pltpu.VMEM((bb, hp + tt, d), F32), pltpu.VMEM((2, bb, hp + tt, cg), F32)],
        compiler_params=_params(2),
        name="pool_mixer",
    )(x, state, g.reshape(1, d), w_grp.astype(BF16), scale.reshape(1, d))


def _sg_kernel(x_ref, g_ref, win_ref, bin_ref, lng_ref, lnb_ref, ws_ref, bs_ref, wout_ref, o_ref, *rest,
               tm, sub, chunk):
    *maybe_v, z_ref = rest
    sg = wout_ref.shape[0]
    cg = sg // SG_GROUPS
    r = lax.broadcasted_iota(jnp.int32, (SG_CHUNK, SG_CHUNK), 0)
    c = lax.broadcasted_iota(jnp.int32, (SG_CHUNK, SG_CHUNK), 1)
    keep = (_idiv(r, chunk) == _idiv(c, chunk)) & (c <= r)
    wss = [jnp.where(keep, ws_ref[gi], 0.0).astype(BF16) for gi in range(SG_GROUPS)]

    def project(i):
        h = _rms_rows(x_ref[i * sub:(i + 1) * sub, :], g_ref[...]).astype(BF16)
        z_ref[i % 2] = _dot(h, win_ref[...])

    def finish(i):
        rows = slice(i * sub, (i + 1) * sub)
        z = _gelu_tanh(z_ref[i % 2] + bin_ref[...])
        u = z[:, :sg]
        v = _layernorm_rows(z[:, sg:], lng_ref[...], lnb_ref[...])
        if maybe_v:
            maybe_v[0][rows, :] = v
        vb = v.astype(BF16)
        gated = []
        for gi in range(SG_GROUPS):
            parts = []
            for r0 in range(0, sub, SG_CHUNK):
                parts.append(_dot(wss[gi], vb[r0:r0 + SG_CHUNK, gi * cg:(gi + 1) * cg]) + bs_ref[gi])
            s = jnp.concatenate(parts, axis=0) if len(parts) > 1 else parts[0]
            gated.append((u[:, gi * cg:(gi + 1) * cg] * s).astype(BF16))
        o_ref[rows, :] = x_ref[rows, :] + _dot(jnp.concatenate(gated, axis=-1), wout_ref[...])

    n_sub = tm // sub
    project(0)
    for i in range(n_sub):
        if i + 1 < n_sub:
            project(i + 1)
        finish(i)


def _sg_mixer(x2d, g, w_in, b_in, ln_g, ln_b, w_s, b_s, w_out, chunk, emit_v, tm):
    n, d = x2d.shape
    sg = w_out.shape[0]
    reps = SG_CHUNK // chunk
    ws = jnp.tile(w_s[:, :chunk, :chunk], (1, reps, reps))
    bs = jnp.tile(b_s[:, :chunk], (1, reps)).reshape(SG_GROUPS, SG_CHUNK, 1)
    row_spec = pl.BlockSpec((tm, d), lambda i: (i, 0))
    out_shape = [jax.ShapeDtypeStruct((n, d), F32)]
    out_specs = [row_spec]
    if emit_v:
        out_shape.append(jax.ShapeDtypeStruct((n, sg), F32))
        out_specs.append(pl.BlockSpec((tm, sg), lambda i: (i, 0)))
    return pl.pallas_call(
        functools.partial(_sg_kernel, tm=tm, sub=min(tm, SG_SUB_ROWS), chunk=chunk),
        grid=(n // tm,),
        in_specs=[row_spec, _const_spec((1, d)), _const_spec((d, 2 * sg)), _const_spec((1, 2 * sg)),
                  _const_spec((1, sg)), _const_spec((1, sg)), _const_spec((SG_GROUPS, SG_CHUNK, SG_CHUNK)),
                  _const_spec((SG_GROUPS, SG_CHUNK, 1)), _const_spec((sg, d))],
        out_specs=out_specs,
        out_shape=out_shape,
        scratch_shapes=[pltpu.VMEM((2, min(tm, SG_SUB_ROWS), 2 * sg), F32)],
        compiler_params=_params(1),
        name="sg_mixer",
    )(x2d, g.reshape(1, d), w_in.astype(BF16), b_in.reshape(1, 2 * sg), ln_g.reshape(1, sg), ln_b.reshape(1, sg),
      ws, bs, w_out.astype(BF16))


def kernel(x_prompt, x_sample, state_conv, cache_k, cache_v, page_table, state_pool, state_ffn, norm_mix, norm_ffn, norm_final, cv_w_in, cv_b_in, cv_w_dw, cv_b_dw, cv_ln_g, cv_ln_b, cv_w_out, cv_b_out, da_w_qkv, da_lq1, da_lk1, da_lq2, da_lk2, da_norm_g, da_w_o, pl_w, pl_scale, sg_w_in, sg_b_in, sg_ln_g, sg_ln_b, sg_w_s, sg_b_s, sg_w_out, ff_w_gate, ff_w_up, ff_w_dw, ff_b_dw, ff_w_down):
    bp, seq, d = x_prompt.shape
    bs, dec_seq, _ = x_sample.shape
    depth, ffn_kw, d_ff = ff_w_dw.shape
    past_len = page_table.shape[1] * cache_k.shape[1]
    head_dim = d // (2 * DA_HEADS)
    kv_shape = (DA_HEADS, 2 * head_dim)

    wg_all, wu_all, wd_all = ff_w_gate.astype(BF16), ff_w_up.astype(BF16), ff_w_down.astype(BF16)

    def ffn(i, x, state, **kw):
        return _ffn(x, state, norm_ffn[i], wg_all, wu_all, ff_w_dw[i], ff_b_dw[i], wd_all, i, **kw)

    ffn_zero = jnp.zeros((bp, ffn_kw - 1, d_ff), F32)
    ffn_p, ffn_s = [], []

    cw = (norm_mix[0], cv_w_in, cv_b_in, cv_w_dw, cv_b_dw, cv_ln_g, cv_ln_b, cv_w_out, cv_b_out)
    xp, conv_p = _conv_mixer(x_prompt, jnp.zeros((bp,) + state_conv.shape[1:], F32), *cw)
    xs, conv_s = _conv_mixer(x_sample, state_conv, *cw)
    xp, st = ffn(0, xp, ffn_zero)
    ffn_p.append(st)
    xs, st = ffn(0, xs, state_ffn[0])
    ffn_s.append(st)

    lam_init = 0.8 - 0.6 * math.exp(-0.3 * 1)
    lqk = (da_lq1, da_lk1, da_lq2, da_lk2)
    scale = head_dim ** -0.5 * math.log2(math.e)
    qt_p, k_rows_p, v_rows_p, kh_p, vt_p = _qkv(xp.reshape(bp * seq, d), norm_mix[1], da_w_qkv, scale, seq=seq)
    o_p = _attn_prompt(qt_p, kh_p, vt_p, lqk, da_norm_g, lam_init, tq=min(seq // ATTN_TILES_PER_STEP, ATTN_TILE),
                       nt=ATTN_TILES_PER_STEP)
    xp, st = ffn(1, xp, ffn_zero, proj=(o_p, da_w_o))
    ffn_p.append(st)
    q_s, k_rows_s, v_rows_s = _qkv(xs.reshape(bs * dec_seq, d), norm_mix[1], da_w_qkv, scale)
    rows_s = (bs, dec_seq) + kv_shape
    o_s = _attn_sample(q_s.reshape(rows_s), k_rows_s.reshape(rows_s), v_rows_s.reshape(rows_s),
                       cache_k, cache_v, page_table, lqk, da_norm_g, lam_init)
    xs, st = ffn(1, xs, state_ffn[1], proj=(o_s.reshape(bs, dec_seq, d), da_w_o))
    ffn_s.append(st)

    xp, pool_p = _pool_mixer(xp, jnp.zeros((bp,) + state_pool.shape[1:], F32), norm_mix[2], pl_w, pl_scale, 0)
    xs, pool_s = _pool_mixer(xs, state_pool, norm_mix[2], pl_w, pl_scale, past_len)
    xp, st = ffn(2, xp, ffn_zero)
    ffn_p.append(st)
    xs, st = ffn(2, xs, state_ffn[2])
    ffn_s.append(st)

    sw = (norm_mix[3], sg_w_in, sg_b_in, sg_ln_g, sg_ln_b, sg_w_s, sg_b_s, sg_w_out)
    assert seq % SG_CHUNK == 0 and past_len % SG_CHUNK == 0 and SG_CHUNK % dec_seq == 0
    (xp2,) = _sg_mixer(xp.reshape(bp * seq, d), *sw, chunk=SG_CHUNK, emit_v=False, tm=min(seq, 2 * SG_SUB_ROWS))
    xs2, sg_v = _sg_mixer(xs.reshape(bs * dec_seq, d), *sw, chunk=dec_seq, emit_v=True, tm=2 * LANES)
    y_prompt, st = ffn(3, xp2.reshape(bp, seq, d), ffn_zero, final_g=norm_final)
    ffn_p.append(st)
    y_sample, st = ffn(3, xs2.reshape(bs, dec_seq, d), state_ffn[3], final_g=norm_final)
    ffn_s.append(st)

    return (y_prompt, y_sample, conv_p, conv_s,
            k_rows_p.reshape((bp, seq) + kv_shape), v_rows_p.reshape((bp, seq) + kv_shape),
            k_rows_s.reshape((bs, dec_seq) + kv_shape), v_rows_s.reshape((bs, dec_seq) + kv_shape),
            pool_p, pool_s, sg_v.reshape(bs, dec_seq, -1), jnp.stack(ffn_p, axis=0), jnp.stack(ffn_s, axis=0))
```

```python
import functools
import math

import jax
import jax.numpy as jnp
from jax import lax
from jax.experimental import pallas as pl
from jax.experimental.pallas import tpu as pltpu

F32 = jnp.float32
BF16 = jnp.bfloat16
NORM_EPS = 1e-6
POOL_WINDOWS = (2, 4, 8, 16)
SG_CHUNK = 128
SG_GROUPS = 4
DA_HEADS = 8

SUBLANES = 8
LANES = 128
MXU_DIM = 256
VMEM_LIMIT_BYTES = 56 * 1024 * 1024
ROWS_PER_STEP = 512
SG_SUB_ROWS = 256
ATTN_TILE = 512
ATTN_TILES_PER_STEP = 8


def _round_up(n, m):
    return -(-n // m) * m


def _plan_bt(batch, seq, short_rows):
    if seq >= ROWS_PER_STEP:
        assert seq % ROWS_PER_STEP == 0
        return 1, ROWS_PER_STEP
    bb = max(1, min(batch, short_rows // seq))
    assert batch % bb == 0 and seq % SUBLANES == 0
    return bb, seq


def _const_spec(shape):
    nd = len(shape)
    return pl.BlockSpec(shape, lambda *_: (0,) * nd, pipeline_mode=pl.Buffered(1))


def _params(n_grid):
    return pltpu.CompilerParams(dimension_semantics=("arbitrary",) * n_grid,
                                vmem_limit_bytes=VMEM_LIMIT_BYTES)


def _rms_rows(x, g):
    return x * lax.rsqrt(jnp.mean(x * x, axis=-1, keepdims=True) + NORM_EPS) * g


def _layernorm_rows(x, g, b):
    mu = jnp.mean(x, axis=-1, keepdims=True)
    xc = x - mu
    return xc * lax.rsqrt(jnp.mean(xc * xc, axis=-1, keepdims=True) + NORM_EPS) * g + b


def _sigmoid(x):
    return 1.0 / (1.0 + jnp.exp(-x))


def _silu(x):
    return x * _sigmoid(x)


def _gelu_tanh(x):
    return x * (0.5 * (1.0 + jnp.tanh(math.sqrt(2.0 / math.pi) * (x + 0.044715 * (x * x * x)))))


def _idiv(x, n):
    assert n & (n - 1) == 0
    return x >> (n.bit_length() - 1)


def _imod(x, n):
    assert n & (n - 1) == 0
    return x & (n - 1)


def _dot(a, b):
    return jnp.dot(a, b, preferred_element_type=F32)


def _diff_lambda(lq1_ref, lk1_ref, lq2_ref, lk2_ref, lam_init):
    a = jnp.sum(lq1_ref[...] * lk1_ref[...], axis=-1, keepdims=True)
    b = jnp.sum(lq2_ref[...] * lk2_ref[...], axis=-1, keepdims=True)
    return jnp.exp(a) - jnp.exp(b) + lam_init


def _conv_mixer_kernel(x_ref, st_ref, g_ref, win_ref, bin_ref, wdw_ref, bdw_ref, lng_ref, lnb_ref,
                       wout_ref, bout_ref, o_ref, sto_ref, gp_ref, c_ref, sh_ref, *, bb, tt, kw, hp, sb, rb, cb):
    hist = kw - 1
    t = pl.program_id(1)
    d = x_ref.shape[-1]
    c = gp_ref.shape[-1]
    rows = bb * tt

    @pl.when(t == 0)
    def _():
        gp_ref[:, hp - hist:hp, :] = st_ref[...]

    x = x_ref[...].reshape(rows, d)
    h = _rms_rows(x, g_ref[...]).astype(BF16)
    ag = _dot(h, win_ref[...]) + bin_ref[...]
    glu = ag[:, :c] * _sigmoid(ag[:, c:])
    gp_ref[:, hp:hp + tt, :] = glu.reshape(bb, tt, c)

    base = hp - hist
    span = sh_ref.shape[2]
    for c0 in range(0, c, cb):
        for r in range(1, SUBLANES):
            sh_ref[r - 1] = gp_ref[:, r:r + span, c0:c0 + cb]
        for b0 in range(0, bb, sb):
            for r0 in range(0, tt, rb):
                acc = jnp.broadcast_to(bdw_ref[:, c0:c0 + cb].reshape(1, 1, cb), (sb, rb, cb))
                for k in range(kw):
                    a, r = divmod(base + k, SUBLANES)
                    lo = r0 + SUBLANES * a
                    if r == 0:
                        win = gp_ref[b0:b0 + sb, lo:lo + rb, c0:c0 + cb]
                    else:
                        win = sh_ref[r - 1, b0:b0 + sb, lo:lo + rb, :]
                    acc = acc + win * wdw_ref[k:k + 1, c0:c0 + cb].reshape(1, 1, cb)
                c_ref[b0:b0 + sb, r0:r0 + rb, c0:c0 + cb] = acc

    conv = c_ref[...].reshape(rows, c)
    act = _silu(_layernorm_rows(conv, lng_ref[...], lnb_ref[...])).astype(BF16)
    m = _dot(act, wout_ref[...]) + bout_ref[...]
    o_ref[...] = (x + m).reshape(bb, tt, d)

    new_hist = gp_ref[:, tt + hp - hist:tt + hp, :]
    gp_ref[:, hp - hist:hp, :] = new_hist

    @pl.when(t == pl.num_programs(1) - 1)
    def _():
        sto_ref[...] = new_hist


def _conv_mixer(x, state, g, w_in, b_in, w_dw, b_dw, ln_g, ln_b, w_out, b_out):
    batch, seq, d = x.shape
    kw, c = w_dw.shape
    hist = kw - 1
    hp = _round_up(hist, SUBLANES)
    bb, tt = _plan_bt(batch, seq, short_rows=ROWS_PER_STEP // 4)
    cb = 2 * LANES
    rb = min(tt, LANES)
    sb = max(1, min(bb, LANES // rb))
    kern = functools.partial(_conv_mixer_kernel, bb=bb, tt=tt, kw=kw, hp=hp, sb=sb, rb=rb, cb=cb)
    x_spec = pl.BlockSpec((bb, tt, d), lambda b, t: (b, t, 0))
    st_spec = pl.BlockSpec((bb, hist, c), lambda b, t: (b, 0, 0))
    return pl.pallas_call(
        kern,
        grid=(batch // bb, seq // tt),
        in_specs=[x_spec, st_spec, _const_spec((1, d)), _const_spec((d, 2 * c)), _const_spec((1, 2 * c)),
                  _const_spec((kw, c)), _const_spec((1, c)), _const_spec((1, c)), _const_spec((1, c)),
                  _const_spec((c, d)), _const_spec((1, d))],
        out_specs=[x_spec, st_spec],
        out_shape=[jax.ShapeDtypeStruct((batch, seq, d), F32), jax.ShapeDtypeStruct((batch, hist, c), F32)],
        scratch_shapes=[pltpu.VMEM((bb, hp + tt, c), F32), pltpu.VMEM((bb, tt, c), F32),
                        pltpu.VMEM((SUBLANES - 1, bb, hp + tt - SUBLANES, cb), F32)],
        compiler_params=_params(2),
        name="conv_mixer",
    )(x, state, g.reshape(1, d), w_in.astype(BF16), b_in.reshape(1, 2 * c), w_dw, b_dw.reshape(1, c),
      ln_g.reshape(1, c), ln_b.reshape(1, c), w_out.astype(BF16), b_out.reshape(1, d))


def _ffn_kernel(*refs, bb, tt, proj, final, chunks, hp):
    refs = list(refs)
    x_ref = refs.pop(0)
    if proj:
        m_ref = refs.pop(0)
        wp_ref = refs.pop(0)
    st_ref, g_ref, wg_ref, wu_ref, wdw_ref, bdw_ref, wd_ref = refs[:7]
    refs = refs[7:]
    if final:
        gf_ref = refs.pop(0)
    o_ref, sto_ref, gpad_ref, carry_ref = refs
    hist = wdw_ref.shape[0] - 1
    t = pl.program_id(1)
    d = x_ref.shape[-1]
    rows = bb * tt

    @pl.when(t == 0)
    def _():
        carry_ref[...] = st_ref[...]

    x = x_ref[...].reshape(rows, d)
    if proj:
        x = x + _dot(m_ref[...].reshape(rows, m_ref.shape[-1]).astype(BF16), wp_ref[...])
    h = _rms_rows(x, g_ref[...]).astype(BF16)
    acc = x
    base = hp - hist
    for c0, fc in chunks:
        gate = _dot(h, wg_ref[:, c0:c0 + fc]).reshape(bb, tt, fc)
        up = _dot(h, wu_ref[:, c0:c0 + fc])
        gpad_ref[:, base:hp, :fc] = carry_ref[:, :, c0:c0 + fc]
        gpad_ref[:, hp:hp + tt, :fc] = gate
        gc = gate * wdw_ref[hist:hist + 1, c0:c0 + fc].reshape(1, 1, fc) + bdw_ref[:, c0:c0 + fc].reshape(1, 1, fc)
        for k in range(hist):
            gc = gc + gpad_ref[:, base + k:base + k + tt, :fc] * wdw_ref[k:k + 1, c0:c0 + fc].reshape(1, 1, fc)
        carry_ref[:, :, c0:c0 + fc] = gpad_ref[:, tt + base:tt + hp, :fc]
        act = (_silu(gc).reshape(rows, fc) * up).astype(BF16)
        acc = acc + _dot(act, wd_ref[c0:c0 + fc, :])
    if final:
        acc = _rms_rows(acc, gf_ref[...])
    o_ref[...] = acc.reshape(bb, tt, d)
    sto_ref[...] = carry_ref[...]


def _ffn(x, state, g, w_gate, w_up, w_dw, b_dw, w_down, layer, proj=None, final_g=None):
    batch, seq, d = x.shape
    kw, f = w_dw.shape

    def layer_spec(shape):
        return pl.BlockSpec((None,) + shape, lambda *_: (layer, 0, 0), pipeline_mode=pl.Buffered(1))

    hist = kw - 1
    hp = _round_up(hist, SUBLANES)
    bb, tt = _plan_bt(batch, seq, short_rows=ROWS_PER_STEP // 2)
    assert f % MXU_DIM == 0
    chunks = ((0, f),)
    kern = functools.partial(_ffn_kernel, bb=bb, tt=tt, proj=proj is not None, final=final_g is not None,
                             chunks=chunks, hp=hp)
    x_spec = pl.BlockSpec((bb, tt, d), lambda b, t: (b, t, 0))
    st_spec = pl.BlockSpec((bb, hist, f), lambda b, t: (b, 0, 0))
    args, specs = [x], [x_spec]
    if proj is not None:
        m, w_proj = proj
        dm = m.shape[-1]
        args += [m, w_proj.astype(BF16)]
        specs += [pl.BlockSpec((bb, tt, dm), lambda b, t: (b, t, 0)), _const_spec((dm, d))]
    args += [state, g.reshape(1, d), w_gate, w_up, w_dw, b_dw.reshape(1, f), w_down]
    specs += [st_spec, _const_spec((1, d)), layer_spec((d, f)), layer_spec((d, f)), _const_spec((kw, f)),
              _const_spec((1, f)), layer_spec((f, d))]
    if final_g is not None:
        args.append(final_g.reshape(1, d))
        specs.append(_const_spec((1, d)))
    return pl.pallas_call(
        kern,
        grid=(batch // bb, seq // tt),
        in_specs=specs,
        out_specs=[x_spec, st_spec],
        out_shape=[jax.ShapeDtypeStruct((batch, seq, d), F32), jax.ShapeDtypeStruct((batch, hist, f), F32)],
        scratch_shapes=[pltpu.VMEM((bb, hp + tt, chunks[0][1]), F32), pltpu.VMEM((bb, hist, f), F32)],
        compiler_params=_params(2),
        name="conv_ffn",
    )(*args)


def _qkv_kernel(x_ref, g_ref, w_ref, q_ref, k_ref, v_ref, *maybe_heads, scale):
    dq = k_ref.shape[-1]
    h = _rms_rows(x_ref[...], g_ref[...]).astype(BF16)
    qkv = _dot(h, w_ref[...])
    q = qkv[:, :dq] * scale
    k = qkv[:, dq:2 * dq]
    v = qkv[:, 2 * dq:]
    k_ref[...] = k
    v_ref[...] = v
    if maybe_heads:
        kh_ref, vt_ref = maybe_heads
        q_ref[...] = q.T.astype(BF16)
        vt_ref[...] = v.T.astype(BF16)
        e = kh_ref.shape[-1]
        for hd in range(kh_ref.shape[0]):
            kh_ref[hd] = k[:, hd * e:(hd + 1) * e].astype(BF16)
    else:
        q_ref[...] = q


def _qkv(x2d, g, w_qkv, scale, seq=None):
    n, d = x2d.shape
    dq = w_qkv.shape[1] // 3
    tm = min(n, ROWS_PER_STEP)
    row_spec = pl.BlockSpec((tm, dq), lambda i: (i, 0))
    rows_f32 = jax.ShapeDtypeStruct((n, dq), F32)
    if seq is None:
        out_shape = [rows_f32] * 3
        out_specs = [row_spec] * 3
    else:
        assert seq % tm == 0
        per_seq = seq // tm
        e = dq // DA_HEADS
        t_shape = jax.ShapeDtypeStruct((n // seq, dq, seq), BF16)
        t_spec = pl.BlockSpec((None, dq, tm), lambda i: (i // per_seq, 0, i % per_seq))
        out_shape = [t_shape, rows_f32, rows_f32, jax.ShapeDtypeStruct((n // seq, DA_HEADS, seq, e), BF16), t_shape]
        out_specs = [t_spec, row_spec, row_spec,
                     pl.BlockSpec((None, DA_HEADS, tm, e), lambda i: (i // per_seq, 0, i % per_seq, 0)), t_spec]
    return pl.pallas_call(
        functools.partial(_qkv_kernel, scale=scale),
        grid=(n // tm,),
        in_specs=[pl.BlockSpec((tm, d), lambda i: (i, 0)), _const_spec((1, d)), _const_spec((d, 3 * dq))],
        out_specs=out_specs,
        out_shape=out_shape,
        compiler_params=_params(1),
        name="qkv",
    )(x2d, g.reshape(1, d), w_qkv.astype(BF16))


def _subln(o, ng, lam_init):
    return o * lax.rsqrt(jnp.mean(o * o, axis=-1, keepdims=True) + NORM_EPS) * ng * (1.0 - lam_init)


def _attn_prompt_kernel(lq1_ref, lk1_ref, lq2_ref, lk2_ref, ng_ref, qt_ref, k_ref, vt_ref, o_ref, sa_ref, sb_ref,
                        *, tq, nt, lam_init):
    j = pl.program_id(2)
    e = qt_ref.shape[0]
    w = 2 * tq
    qt = qt_ref[...].astype(F32)
    row = lax.broadcasted_iota(jnp.int32, qt.shape, 0)
    lo = jnp.where(row < e // 2, qt, 0.0)
    hi = jnp.where(row >= e // 2, qt, 0.0)
    qq = jnp.concatenate([half[:, g * tq:(g + 1) * tq] for g in range(nt) for half in (lo, hi)],
                         axis=1).astype(BF16)

    def scores(c, dst_ref, first_col):
        start = pl.multiple_of(c * tq, tq)
        dst_ref[:, first_col:] = _dot(k_ref[pl.ds(start, tq), :], qq[:, first_col:])

    def absorb(src_ref, c, carry, tile, own_chunk):
        m, l, acc = carry
        start = pl.multiple_of(c * tq, tq)
        s = src_ref[:, tile * w:(tile + 1) * w]
        if own_chunk:
            key = lax.broadcasted_iota(jnp.int32, s.shape, 0)
            col = lax.broadcasted_iota(jnp.int32, s.shape, 1)
            s = jnp.where(key <= jnp.where(col >= tq, col - tq, col), s, -jnp.inf)
        m_new = jnp.maximum(m, jnp.max(s, axis=0, keepdims=True))
        p = jnp.exp2(s - m_new)
        alpha = jnp.exp2(m - m_new)
        l = alpha * l + jnp.sum(p, axis=0, keepdims=True)
        return m_new, l, alpha * acc + _dot(vt_ref[:, pl.ds(start, tq)], p.astype(BF16))

    init = (jnp.full((1, w), -jnp.inf, F32), jnp.zeros((1, w), F32), jnp.zeros((e, w), F32))
    scores(0, sa_ref, 0)

    def two_chunks(i, carries):
        scores(2 * i + 1, sb_ref, 0)
        carries = tuple(absorb(sa_ref, 2 * i, c, g, False) for g, c in enumerate(carries))
        scores(2 * i + 2, sa_ref, 0)
        return tuple(absorb(sb_ref, 2 * i + 1, c, g, False) for g, c in enumerate(carries))

    assert nt % 2 == 0
    first = nt * j
    carries = list(lax.fori_loop(0, (nt // 2) * j, two_chunks, (init,) * nt))
    bufs = (sa_ref, sb_ref)
    for d in range(nt):
        if d + 1 < nt:
            scores(first + d + 1, bufs[(d + 1) % 2], (d + 1) * w)
        for g in range(d, nt):
            carries[g] = absorb(bufs[d % 2], first + d, carries[g], g, g == d)

    lam = _diff_lambda(lq1_ref, lk1_ref, lq2_ref, lk2_ref, lam_init)
    for tile, (_, l, acc) in enumerate(carries):
        o = acc * (1.0 / l)
        o = (o[:, :tq] - lam * o[:, tq:]).T
        o_ref[tile * tq:(tile + 1) * tq, :] = _subln(o, ng_ref[...], lam_init).astype(o_ref.dtype)


def _attn_prompt(qt, k, vt, lqk, norm_g, lam_init, tq, nt):
    batch, dq, seq = qt.shape
    e = dq // DA_HEADS
    dl = lqk[0].shape[0]
    assert seq % (nt * tq) == 0
    qt_spec = pl.BlockSpec((None, e, nt * tq), lambda b, h, i: (b, h, i))
    k_spec = pl.BlockSpec((None, None, seq, e), lambda b, h, i: (b, h, 0, 0))
    vt_spec = pl.BlockSpec((None, e, seq), lambda b, h, i: (b, h, 0))
    return pl.pallas_call(
        functools.partial(_attn_prompt_kernel, tq=tq, nt=nt, lam_init=lam_init),
        grid=(batch, DA_HEADS, seq // (nt * tq)),
        in_specs=[_const_spec((1, dl))] * 4 + [pl.BlockSpec((1, e), lambda b, h, i: (0, h)), qt_spec, k_spec, vt_spec],
        out_specs=pl.BlockSpec((None, nt * tq, e), lambda b, h, i: (b, i, h)),
        out_shape=jax.ShapeDtypeStruct((batch, seq, dq), BF16),
        scratch_shapes=[pltpu.VMEM((tq, 2 * nt * tq), F32), pltpu.VMEM((tq, 2 * nt * tq), F32)],
        compiler_params=_params(3),
        name="attn_prompt",
    )(*[a.reshape(1, dl) for a in lqk], norm_g.reshape(1, dq), qt, k, vt)


def _attn_sample_kernel(pt_ref, lq1_ref, lk1_ref, lq2_ref, lk2_ref, ng_ref, q_ref, kn_ref, vn_ref, ck_ref, cv_ref,
                        o_ref, s_ref, buf_ref, sem_ref, *, n_pages, lam_init):
    b = pl.program_id(0)
    nb = pl.num_programs(0)
    t, nh, e = q_ref.shape
    page = buf_ref.shape[2]

    def page_copy(src_ref, seq, p, slot):
        return pltpu.make_async_copy(src_ref.at[pt_ref[seq, p]], buf_ref.at[slot, p], sem_ref.at[slot])

    def start_set(src_ref, seq, slot):
        for p in range(n_pages):
            page_copy(src_ref, seq, p, slot).start(priority=p % 2)

    def wait_set(src_ref, seq, slot):
        for p in range(n_pages):
            page_copy(src_ref, seq, p, slot).wait()

    k_slot = 2 * (b % 2)
    v_slot = k_slot + 1
    next_v_slot = 3 - k_slot

    @pl.when(b == 0)
    def _():
        start_set(ck_ref, 0, 0)
        start_set(cv_ref, 0, 1)

    @pl.when((b == 0) & (nb > 1))
    def _():
        start_set(ck_ref, 1, 2)

    wait_set(ck_ref, b, k_slot)

    @pl.when(b + 1 < nb)
    def _():
        start_set(cv_ref, b + 1, next_v_slot)

    k_pages = [buf_ref.at[k_slot, p] for p in range(n_pages)]
    v_pages = [buf_ref.at[v_slot, p] for p in range(n_pages)]
    rows = t * nh
    cols = page * nh
    nt = (((1,), (1,)), ((), ()))

    q2 = q_ref[...].reshape(rows, e)
    lane = lax.broadcasted_iota(jnp.int32, q2.shape, 1)
    wq = jnp.concatenate([jnp.where(lane < e // 2, q2, 0.0), jnp.where(lane >= e // 2, q2, 0.0)], axis=0).astype(BF16)

    r = lax.broadcasted_iota(jnp.int32, (2 * rows, cols), 0)
    c = lax.broadcasted_iota(jnp.int32, (2 * rows, cols), 1)
    same_head = _imod(c - r, nh) == 0

    rn = lax.broadcasted_iota(jnp.int32, (2 * rows, rows), 0)
    cn = lax.broadcasted_iota(jnp.int32, (2 * rows, rows), 1)
    keep_new = (_imod(cn - rn, nh) == 0) & (_idiv(cn, nh) <= _idiv(_imod(rn, rows), nh))
    s_new = lax.dot_general(wq, kn_ref[...].reshape(rows, e).astype(BF16), nt, preferred_element_type=F32)
    s_new = jnp.where(keep_new, s_new, -jnp.inf)

    mx = None
    for p in range(n_pages):
        k2 = k_pages[p][...].reshape(cols, e).astype(BF16)
        s = lax.dot_general(wq, k2, nt, preferred_element_type=F32)
        s_ref[:, p * cols:(p + 1) * cols] = s
        mx = s if mx is None else jnp.maximum(mx, s)
    mx = jnp.where(same_head, mx, -jnp.inf)
    m = jnp.maximum(jnp.max(mx, axis=-1, keepdims=True), jnp.max(s_new, axis=-1, keepdims=True))

    wait_set(cv_ref, b, v_slot)

    @pl.when(b + 2 < nb)
    def _():
        start_set(ck_ref, b + 2, k_slot)

    shift = jnp.where(same_head, -m, -jnp.inf)
    p_new = jnp.exp2(s_new - m)
    acc = _dot(p_new.astype(BF16), vn_ref[...].reshape(rows, e).astype(BF16))
    tot = None
    for p in range(n_pages):
        pe = jnp.exp2(s_ref[:, p * cols:(p + 1) * cols] + shift)
        tot = pe if tot is None else tot + pe
        acc = acc + _dot(pe.astype(BF16), v_pages[p][...].reshape(cols, e).astype(BF16))
    l = jnp.sum(tot, axis=-1, keepdims=True) + jnp.sum(p_new, axis=-1, keepdims=True)

    o = acc * (1.0 / l)
    lam = _diff_lambda(lq1_ref, lk1_ref, lq2_ref, lk2_ref, lam_init)
    o = o[:rows] - lam * o[rows:]
    ng = jnp.concatenate([ng_ref[...]] * t, axis=0)
    o_ref[...] = _subln(o, ng, lam_init).reshape(t, nh, e)


def _attn_sample(q, k_new, v_new, cache_k, cache_v, page_table, lqk, norm_g, lam_init):
    batch, t, nh, e = q.shape
    page = cache_k.shape[1]
    n_pages = page_table.shape[1]
    dl = lqk[0].shape[0]
    row_spec = pl.BlockSpec((None, t, nh, e), lambda b, pt: (b, 0, 0, 0))

    hbm_spec = pl.BlockSpec(memory_space=pl.ANY)
    grid_spec = pltpu.PrefetchScalarGridSpec(
        num_scalar_prefetch=1,
        grid=(batch,),
        in_specs=[_const_spec((1, dl))] * 4 + [_const_spec((nh, e)), row_spec, row_spec, row_spec, hbm_spec, hbm_spec],
        out_specs=row_spec,
        scratch_shapes=[pltpu.VMEM((2 * t * nh, n_pages * page * nh), F32),
                        pltpu.VMEM((4, n_pages, page, nh, e), F32),
                        pltpu.SemaphoreType.DMA((4,))],
    )
    return pl.pallas_call(
        functools.partial(_attn_sample_kernel, n_pages=n_pages, lam_init=lam_init),
        grid_spec=grid_spec,
        out_shape=jax.ShapeDtypeStruct((batch, t, nh, e), F32),
        compiler_params=_params(1),
        name="attn_sample",
    )(page_table, *[a.reshape(1, dl) for a in lqk], norm_g.reshape(nh, e), q, k_new, v_new, cache_k, cache_v)


def _pool_kernel(x_ref, st_ref, g_ref, w_ref, sc_ref, o_ref, sto_ref, hp_ref, lv_ref, *, bb, tt, windows, pos0,
                 hist, hp):
    t = pl.program_id(1)
    d = x_ref.shape[-1]
    cg = d // len(windows)
    rows = bb * tt
    lead = SUBLANES

    @pl.when(t == 0)
    def _():
        hp_ref[:, :hp - hist, :] = jnp.zeros((bb, hp - hist, d), F32)
        hp_ref[:, hp - hist:hp, :] = st_ref[...]
        lv_ref[:, :, :lead, :] = jnp.zeros((2, bb, lead, cg), F32)

    x = x_ref[...].reshape(rows, d)
    h = _rms_rows(x, g_ref[...]).reshape(bb, tt, d)
    hp_ref[:, hp:hp + tt, :] = h
    pos = pos0 + t * tt + lax.broadcasted_iota(jnp.int32, (1, tt, 1), 1)
    ys = []
    for gi, win in enumerate(windows):
        assert win & (win - 1) == 0 and win <= hp - lead
        c0 = gi * cg
        cur = h[:, :, c0:c0 + cg]

        def level_rows(level, lo, hi):
            if level == 0:
                return hp_ref[:, lo:hi, c0:c0 + cg]
            return lv_ref[level % 2, :, lo:hi, :]

        n_levels = win.bit_length() - 1
        for level in range(n_levels - 1):
            w = 1 << level
            lv_ref[(level + 1) % 2, :, lead:hp + tt, :] = (level_rows(level, lead, hp + tt)
                                                            + level_rows(level, lead - w, hp + tt - w))
        w = win // 2
        tot = level_rows(n_levels - 1, hp, hp + tt) + level_rows(n_levels - 1, hp - w, hp + tt - w)
        cnt = jnp.minimum(win, pos + 1).astype(F32)
        pooled = (tot / cnt - cur).reshape(rows, cg).astype(BF16)
        ys.append(_dot(pooled, w_ref[gi]))
    y = jnp.concatenate(ys, axis=-1) * sc_ref[...]
    o_ref[...] = (x + y).reshape(bb, tt, d)

    new_hist = hp_ref[:, tt + hp - hist:tt + hp, :]
    hp_ref[:, hp - hist:hp, :] = new_hist

    @pl.when(t == pl.num_programs(1) - 1)
    def _():
        sto_ref[...] = new_hist


def _pool_mixer(x, state, g, w_grp, scale, pos0):
    batch, seq, d = x.shape
    hist = max(POOL_WINDOWS) - 1
    hp = _round_up(hist, SUBLANES) + SUBLANES
    n_grp, cg, _ = w_grp.shape
    bb, tt = _plan_bt(batch, seq, short_rows=ROWS_PER_STEP // 2)
    x_spec = pl.BlockSpec((bb, tt, d), lambda b, t: (b, t, 0))
    st_spec = pl.BlockSpec((bb, hist, d), lambda b, t: (b, 0, 0))
    return pl.pallas_call(
        functools.partial(_pool_kernel, bb=bb, tt=tt, windows=POOL_WINDOWS, pos0=pos0, hist=hist, hp=hp),
        grid=(batch // bb, seq // tt),
        in_specs=[x_spec, st_spec, _const_spec((1, d)), _const_spec((n_grp, cg, cg)), _const_spec((1, d))],
        out_specs=[x_spec, st_spec],
        out_shape=[jax.ShapeDtypeStruct((batch, seq, d), F32), jax.ShapeDtypeStruct((batch, hist, d), F32)],
        scratch_shapes=[pltpu.VMEM((bb, hp + tt, d), F32), pltpu.VMEM((2, bb, hp + tt, cg), F32)],
        compiler_params=_params(2),
        name="pool_mixer",
    )(x, state, g.reshape(1, d), w_grp.astype(BF16), scale.reshape(1, d))


def _sg_kernel(x_ref, g_ref, win_ref, bin_ref, lng_ref, lnb_ref, ws_ref, bs_ref, wout_ref, o_ref, *rest,
               tm, sub, chunk):
    *maybe_v, z_ref = rest
    sg = wout_ref.shape[0]
    cg = sg // SG_GROUPS
    r = lax.broadcasted_iota(jnp.int32, (SG_CHUNK, SG_CHUNK), 0)
    c = lax.broadcasted_iota(jnp.int32, (SG_CHUNK, SG_CHUNK), 1)
    keep = (_idiv(r, chunk) == _idiv(c, chunk)) & (c <= r)
    wss = [jnp.where(keep, ws_ref[gi], 0.0).astype(BF16) for gi in range(SG_GROUPS)]

    def project(i):
        h = _rms_rows(x_ref[i * sub:(i + 1) * sub, :], g_ref[...]).astype(BF16)
        z_ref[i % 2] = _dot(h, win_ref[...])

    def finish(i):
        rows = slice(i * sub, (i + 1) * sub)
        z = _gelu_tanh(z_ref[i % 2] + bin_ref[...])
        u = z[:, :sg]
        v = _layernorm_rows(z[:, sg:], lng_ref[...], lnb_ref[...])
        if maybe_v:
            maybe_v[0][rows, :] = v
        vb = v.astype(BF16)
        gated = []
        for gi in range(SG_GROUPS):
            parts = []
            for r0 in range(0, sub, SG_CHUNK):
                parts.append(_dot(wss[gi], vb[r0:r0 + SG_CHUNK, gi * cg:(gi + 1) * cg]) + bs_ref[gi])
            s = jnp.concatenate(parts, axis=0) if len(parts) > 1 else parts[0]
            gated.append((u[:, gi * cg:(gi + 1) * cg] * s).astype(BF16))
        o_ref[rows, :] = x_ref[rows, :] + _dot(jnp.concatenate(gated, axis=-1), wout_ref[...])

    n_sub = tm // sub
    project(0)
    for i in range(n_sub):
        if i + 1 < n_sub:
            project(i + 1)
        finish(i)


def _sg_mixer(x2d, g, w_in, b_in, ln_g, ln_b, w_s, b_s, w_out, chunk, emit_v, tm):
    n, d = x2d.shape
    sg = w_out.shape[0]
    reps = SG_CHUNK // chunk
    ws = jnp.tile(w_s[:, :chunk, :chunk], (1, reps, reps))
    bs = jnp.tile(b_s[:, :chunk], (1, reps)).reshape(SG_GROUPS, SG_CHUNK, 1)
    row_spec = pl.BlockSpec((tm, d), lambda i: (i, 0))
    out_shape = [jax.ShapeDtypeStruct((n, d), F32)]
    out_specs = [row_spec]
    if emit_v:
        out_shape.append(jax.ShapeDtypeStruct((n, sg), F32))
        out_specs.append(pl.BlockSpec((tm, sg), lambda i: (i, 0)))
    return pl.pallas_call(
        functools.partial(_sg_kernel, tm=tm, sub=min(tm, SG_SUB_ROWS), chunk=chunk),
        grid=(n // tm,),
        in_specs=[row_spec, _const_spec((1, d)), _const_spec((d, 2 * sg)), _const_spec((1, 2 * sg)),
                  _const_spec((1, sg)), _const_spec((1, sg)), _const_spec((SG_GROUPS, SG_CHUNK, SG_CHUNK)),
                  _const_spec((SG_GROUPS, SG_CHUNK, 1)), _const_spec((sg, d))],
        out_specs=out_specs,
        out_shape=out_shape,
        scratch_shapes=[pltpu.VMEM((2, min(tm, SG_SUB_ROWS), 2 * sg), F32)],
        compiler_params=_params(1),
        name="sg_mixer",
    )(x2d, g.reshape(1, d), w_in.astype(BF16), b_in.reshape(1, 2 * sg), ln_g.reshape(1, sg), ln_b.reshape(1, sg),
      ws, bs, w_out.astype(BF16))


def kernel(x_prompt, x_sample, state_conv, cache_k, cache_v, page_table, state_pool, state_ffn, norm_mix, norm_ffn, norm_final, cv_w_in, cv_b_in, cv_w_dw, cv_b_dw, cv_ln_g, cv_ln_b, cv_w_out, cv_b_out, da_w_qkv, da_lq1, da_lk1, da_lq2, da_lk2, da_norm_g, da_w_o, pl_w, pl_scale, sg_w_in, sg_b_in, sg_ln_g, sg_ln_b, sg_w_s, sg_b_s, sg_w_out, ff_w_gate, ff_w_up, ff_w_dw, ff_b_dw, ff_w_down):
    bp, seq, d = x_prompt.shape
    bs, dec_seq, _ = x_sample.shape
    depth, ffn_kw, d_ff = ff_w_dw.shape
    past_len = page_table.shape[1] * cache_k.shape[1]
    head_dim = d // (2 * DA_HEADS)
    kv_shape = (DA_HEADS, 2 * head_dim)

    wg_all, wu_all, wd_all = ff_w_gate.astype(BF16), ff_w_up.astype(BF16), ff_w_down.astype(BF16)

    def ffn(i, x, state, **kw):
        return _ffn(x, state, norm_ffn[i], wg_all, wu_all, ff_w_dw[i], ff_b_dw[i], wd_all, i, **kw)

    ffn_zero = jnp.zeros((bp, ffn_kw - 1, d_ff), F32)
    ffn_p, ffn_s = [], []

    cw = (norm_mix[0], cv_w_in, cv_b_in, cv_w_dw, cv_b_dw, cv_ln_g, cv_ln_b, cv_w_out, cv_b_out)
    xp, conv_p = _conv_mixer(x_prompt, jnp.zeros((bp,) + state_conv.shape[1:], F32), *cw)
    xs, conv_s = _conv_mixer(x_sample, state_conv, *cw)
    xp, st = ffn(0, xp, ffn_zero)
    ffn_p.append(st)
    xs, st = ffn(0, xs, state_ffn[0])
    ffn_s.append(st)

    lam_init = 0.8 - 0.6 * math.exp(-0.3 * 1)
    lqk = (da_lq1, da_lk1, da_lq2, da_lk2)
    scale = head_dim ** -0.5 * math.log2(math.e)
    qt_p, k_rows_p, v_rows_p, kh_p, vt_p = _qkv(xp.reshape(bp * seq, d), norm_mix[1], da_w_qkv, scale, seq=seq)
    o_p = _attn_prompt(qt_p, kh_p, vt_p, lqk, da_norm_g, lam_init, tq=min(seq // ATTN_TILES_PER_STEP, ATTN_TILE),
                       nt=ATTN_TILES_PER_STEP)
    xp, st = ffn(1, xp, ffn_zero, proj=(o_p, da_w_o))
    ffn_p.append(st)
    q_s, k_rows_s, v_rows_s = _qkv(xs.reshape(bs * dec_seq, d), norm_mix[1], da_w_qkv, scale)
    rows_s = (bs, dec_seq) + kv_shape
    o_s = _attn_sample(q_s.reshape(rows_s), k_rows_s.reshape(rows_s), v_rows_s.reshape(rows_s),
                       cache_k, cache_v, page_table, lqk, da_norm_g, lam_init)
    xs, st = ffn(1, xs, state_ffn[1], proj=(o_s.reshape(bs, dec_seq, d), da_w_o))
    ffn_s.append(st)

    xp, pool_p = _pool_mixer(xp, jnp.zeros((bp,) + state_pool.shape[1:], F32), norm_mix[2], pl_w, pl_scale, 0)
    xs, pool_s = _pool_mixer(xs, state_pool, norm_mix[2], pl_w, pl_scale, past_len)
    xp, st = ffn(2, xp, ffn_zero)
    ffn_p.append(st)
    xs, st = ffn(2, xs, state_ffn[2])
    ffn_s.append(st)

    sw = (norm_mix[3], sg_w_in, sg_b_in, sg_ln_g, sg_ln_b, sg_w_s, sg_b_s, sg_w_out)
    assert seq % SG_CHUNK == 0 and past_len % SG_CHUNK == 0 and SG_CHUNK % dec_seq == 0
    (xp2,) = _sg_mixer(xp.reshape(bp * seq, d), *sw, chunk=SG_CHUNK, emit_v=False, tm=min(seq, 2 * SG_SUB_ROWS))
    xs2, sg_v = _sg_mixer(xs.reshape(bs * dec_seq, d), *sw, chunk=dec_seq, emit_v=True, tm=2 * LANES)
    y_prompt, st = ffn(3, xp2.reshape(bp, seq, d), ffn_zero, final_g=norm_final)
    ffn_p.append(st)
    y_sample, st = ffn(3, xs2.reshape(bs, dec_seq, d), state_ffn[3], final_g=norm_final)
    ffn_s.append(st)

    return (y_prompt, y_sample, conv_p, conv_s,
            k_rows_p.reshape((bp, seq) + kv_shape), v_rows_p.reshape((bp, seq) + kv_shape),
            k_rows_s.reshape((bs, dec_seq) + kv_shape), v_rows_s.reshape((bs, dec_seq) + kv_shape),
            pool_p, pool_s, sg_v.reshape(bs, dec_seq, -1), jnp.stack(ffn_p, axis=0), jnp.stack(ffn_s, axis=0))
```
